```python
import math
import jax
import jax.numpy as jnp
from jax import lax
import numpy as np

D_MODEL = 1024
BATCH = 4
SEQ = 4096
DEPTH = 2
DEC_BATCH = 32
DEC_SEQ = 4
PAST_LEN = 16384
PAGE_SIZE = 128

N_MIXERS = 2
N_ATTN_LAYERS = (DEPTH + N_MIXERS - 1) // N_MIXERS
N_MLSTM_LAYERS = DEPTH // N_MIXERS
DA_HEADS = 8
DA_HD = D_MODEL // (2 * DA_HEADS)
DA_VD = 2 * DA_HD
DA_QKV = 4 * DA_HEADS * DA_HD + DA_HEADS * DA_VD
ROPE_THETA = 10000.0
Q_BLOCK = 128
ML_HEADS = 8
ML_DK = D_MODEL // (2 * ML_HEADS)
ML_DV = D_MODEL // ML_HEADS
ML_CHUNK = 128
GATE_SOFTCAP = 15.0
ML_IN = 2 * ML_HEADS * ML_DK + ML_HEADS * ML_DV + D_MODEL + 2 * ML_HEADS
FFN_HIDDEN = -(-8 * D_MODEL // (3 * 256)) * 256
RMS_EPS = 1e-6

kernel_name = 'hybrid_diffattn_mlstm_step'


def _rmsnorm(x, g):
    xf = x.astype(jnp.float32)
    y = xf * lax.rsqrt(jnp.mean(xf * xf, axis=-1, keepdims=True) + RMS_EPS)
    return (y * g.astype(jnp.float32)).astype(x.dtype)


def _rope(x, pos):
    hd = x.shape[-1]
    half = hd // 2
    inv = ROPE_THETA ** (-jnp.arange(half, dtype=jnp.float32) * 2.0 / hd)
    ang = pos[:, None] * inv[None, :]
    cos = jnp.cos(ang)[None, :, None, :]
    sin = jnp.sin(ang)[None, :, None, :]
    xf = x.astype(jnp.float32)
    x1, x2 = xf[..., :half], xf[..., half:]
    return jnp.concatenate([x1 * cos - x2 * sin, x2 * cos + x1 * sin], axis=-1).astype(x.dtype)


def _diff_attn_project(h, w_qkv, pos):
    B, T, _ = h.shape
    qkv = h @ w_qkv
    nq = 2 * DA_HEADS * DA_HD
    q, k, v = jnp.split(qkv, [nq, 2 * nq], axis=-1)
    q = _rope(q.reshape(B, T, 2 * DA_HEADS, DA_HD), pos)
    k = _rope(k.reshape(B, T, 2 * DA_HEADS, DA_HD), pos)
    v = v.reshape(B, T, DA_HEADS, DA_VD)
    return q, k, v


def _prompt_two_map_attention(q, k, v):
    B, T = q.shape[:2]
    qb = min(Q_BLOCK, T)
    nb = T // qb
    scale = DA_HD ** -0.5
    qh = q.reshape(B, nb, qb, DA_HEADS, 2, DA_HD).transpose(1, 0, 2, 3, 4, 5)
    kh = k.reshape(B, T, DA_HEADS, 2, DA_HD)
    key_pos = jnp.arange(T)

    def block(args):
        qblk, bi = args
        s = jnp.einsum('bqhjd,bkhjd->bhjqk', qblk, kh, preferred_element_type=jnp.float32) * scale
        qpos = bi * qb + jnp.arange(qb)
        s = jnp.where(qpos[:, None] >= key_pos[None, :], s, -jnp.inf)
        p = jax.nn.softmax(s, axis=-1)
        return jnp.einsum('bhjqk,bkhv->bqhjv', p.astype(v.dtype), v)

    o = lax.map(block, (qh, jnp.arange(nb)))
    return o.transpose(1, 0, 2, 3, 4, 5).reshape(B, T, DA_HEADS, 2, DA_VD)


def _online_update(carry, s, v):
    m, l, acc = carry
    m_new = jnp.maximum(m, s.max(axis=-1))
    alpha = jnp.exp(m - m_new)
    p = jnp.exp(s - m_new[..., None])
    l = l * alpha + p.sum(axis=-1)
    acc = acc * alpha[..., None] + jnp.einsum('bhjqk,bkhv->bhjqv', p, v.astype(jnp.float32))
    return m_new, l, acc


def _paged_two_map_attention(q, k_new, v_new, cache_k, cache_v, page_table):
    B, Tn = q.shape[:2]
    scale = DA_HD ** -0.5
    qh = q.reshape(B, Tn, DA_HEADS, 2, DA_HD)
    ck = cache_k.reshape(cache_k.shape[0], PAGE_SIZE, DA_HEADS, 2, DA_HD)
    init = (jnp.full((B, DA_HEADS, 2, Tn), -jnp.inf, jnp.float32),
            jnp.zeros((B, DA_HEADS, 2, Tn), jnp.float32),
            jnp.zeros((B, DA_HEADS, 2, Tn, DA_VD), jnp.float32))

    def page_step(carry, pidx):
        kp = ck[pidx]
        vp = cache_v[pidx]
        s = jnp.einsum('bqhjd,bkhjd->bhjqk', qh, kp, preferred_element_type=jnp.float32) * scale
        return _online_update(carry, s, vp), None

    carry, _ = lax.scan(page_step, init, page_table.T)
    kn = k_new.reshape(B, Tn, DA_HEADS, 2, DA_HD)
    s = jnp.einsum('bqhjd,bkhjd->bhjqk', qh, kn, preferred_element_type=jnp.float32) * scale
    causal = jnp.tril(jnp.ones((Tn, Tn), dtype=bool))
    s = jnp.where(causal, s, -jnp.inf)
    _, l, acc = _online_update(carry, s, v_new)
    o = acc / l[..., None]
    return o.transpose(0, 3, 1, 2, 4).astype(q.dtype)


def _diff_attn_output(o, lam_params, lam_init, head_gain, w_o):
    lp = lam_params.astype(jnp.float32)
    lam = jnp.exp(jnp.sum(lp[0] * lp[1])) - jnp.exp(jnp.sum(lp[2] * lp[3])) + lam_init
    of = o.astype(jnp.float32)
    d = of[..., 0, :] - lam * of[..., 1, :]
    d = d * lax.rsqrt(jnp.mean(d * d, axis=-1, keepdims=True) + RMS_EPS)
    d = d * head_gain.astype(jnp.float32) * (1.0 - lam_init)
    B, T = d.shape[:2]
    return d.reshape(B, T, D_MODEL).astype(o.dtype) @ w_o


def _mlstm_project(h, w_in, b_gates):
    B, T, _ = h.shape
    z = h @ w_in
    hk = ML_HEADS * ML_DK
    hv = ML_HEADS * ML_DV
    q, k, v, og, g = jnp.split(z, [hk, 2 * hk, 2 * hk + hv, 2 * hk + hv + D_MODEL], axis=-1)
    g = g.astype(jnp.float32) + b_gates.astype(jnp.float32)
    ig = GATE_SOFTCAP * jnp.tanh(g[..., :ML_HEADS] / GATE_SOFTCAP)
    lf = jax.nn.log_sigmoid(g[..., ML_HEADS:])
    return (q.reshape(B, T, ML_HEADS, ML_DK), k.reshape(B, T, ML_HEADS, ML_DK),
            v.reshape(B, T, ML_HEADS, ML_DV), og, ig, lf)


def _mlstm_chunkwise(q, k, v, ig, lf, C0, n0, m0):
    B, T = q.shape[:2]
    L = ML_CHUNK if T % ML_CHUNK == 0 else T
    nc = T // L

    def chunks(a):
        a = a.astype(jnp.float32).reshape((B, nc, L) + a.shape[2:])
        return jnp.moveaxis(a, (1, 3), (0, 2))

    tril = jnp.tril(jnp.ones((L, L), dtype=bool))

    def step(carry, xs):
        C, n, m = carry
        qc, kc, vc, ic, fc = xs
        b = jnp.cumsum(fc, axis=-1)
        logD = jnp.where(tril, b[..., :, None] - b[..., None, :] + ic[..., None, :], -jnp.inf)
        inter = b + m[..., None]
        mt = jnp.maximum(inter, logD.max(axis=-1))
        Dm = jnp.exp(logD - mt[..., None])
        wi = jnp.exp(inter - mt)
        sqk = jnp.einsum('bhtd,bhsd->bhts', qc, kc) * Dm
        num = wi[..., None] * jnp.einsum('bhtd,bhdv->bhtv', qc, C) + jnp.einsum('bhts,bhsv->bhtv', sqk, vc)
        den = wi * jnp.einsum('bhtd,bhd->bht', qc, n) + sqk.sum(axis=-1)
        hc = num / jnp.maximum(jnp.abs(den), jnp.exp(-mt))[..., None]
        bL = b[..., -1]
        ls = bL[..., None] - b + ic
        m_new = jnp.maximum(bL + m, ls.max(axis=-1))
        ws = jnp.exp(ls - m_new[..., None])
        wc = jnp.exp(bL + m - m_new)
        C_new = wc[..., None, None] * C + jnp.einsum('bhs,bhsd,bhsv->bhdv', ws, kc, vc)
        n_new = wc[..., None] * n + jnp.einsum('bhs,bhsd->bhd', ws, kc)
        return (C_new, n_new, m_new), hc

    xs = (chunks(q) * (ML_DK ** -0.5), chunks(k), chunks(v), chunks(ig), chunks(lf))
    init = (C0.astype(jnp.float32), n0.astype(jnp.float32), m0.astype(jnp.float32))
    state, hs = lax.scan(step, init, xs)
    hs = jnp.moveaxis(hs, (0, 2), (1, 3)).reshape(B, T, ML_HEADS, ML_DV)
    return hs, state


def _mlstm_output(hs, og, head_gain, w_out):
    B, T = hs.shape[:2]
    hn = hs * lax.rsqrt(jnp.mean(hs * hs, axis=-1, keepdims=True) + RMS_EPS) * head_gain.astype(jnp.float32)
    y = hn.reshape(B, T, D_MODEL) * jax.nn.sigmoid(og.astype(jnp.float32))
    return y.astype(og.dtype) @ w_out


def _swiglu(h, wg, wu, wd):
    return (jax.nn.silu(h @ wg) * (h @ wu)) @ wd


def setup_inputs(seed: int = 0) -> dict:
    key = jax.random.key(seed)
    ks = jax.random.split(key, 24)
    f32 = jnp.float32
    n_pages = PAST_LEN // PAGE_SIZE
    n_used = DEC_BATCH * n_pages
    n_pool = n_used + -(-n_used // 4)

    def nrm(k, shape, scale):
        return jax.random.normal(k, shape, f32) * scale

    def gain(k, shape):
        return 1.0 + nrm(k, shape, 0.05)

    page_table = jax.random.permutation(ks[2], n_pool)[:n_used].reshape(DEC_BATCH, n_pages).astype(jnp.int32)
    b_gates = jnp.concatenate(
        [nrm(ks[15], (N_MLSTM_LAYERS, ML_HEADS), 0.1),
         jnp.linspace(3.0, 6.0, ML_HEADS, dtype=f32)[None, :] + nrm(ks[16], (N_MLSTM_LAYERS, ML_HEADS), 0.1)],
        axis=-1)
    return {
        'x_prompt': nrm(ks[0], (BATCH, SEQ, D_MODEL), 1.0),
        'x_sample': nrm(ks[1], (DEC_BATCH, DEC_SEQ, D_MODEL), 1.0),
        'cache_k': nrm(ks[3], (N_ATTN_LAYERS, n_pool, PAGE_SIZE, 2 * DA_HEADS, DA_HD), 1.0),
        'cache_v': nrm(ks[4], (N_ATTN_LAYERS, n_pool, PAGE_SIZE, DA_HEADS, DA_VD), 1.0),
        'page_table': page_table,
        'state_C': nrm(ks[5], (N_MLSTM_LAYERS, DEC_BATCH, ML_HEADS, ML_DK, ML_DV), 0.1),
        'state_n': nrm(ks[6], (N_MLSTM_LAYERS, DEC_BATCH, ML_HEADS, ML_DK), 0.1),
        'state_m': nrm(ks[7], (N_MLSTM_LAYERS, DEC_BATCH, ML_HEADS), 1.0),
        'attn_norm': gain(ks[8], (N_ATTN_LAYERS, D_MODEL)),
        'w_attn_qkv': nrm(ks[9], (N_ATTN_LAYERS, D_MODEL, DA_QKV), D_MODEL ** -0.5),
        'attn_lambda': nrm(ks[10], (N_ATTN_LAYERS, 4, DA_HD), 0.1),
        'attn_head_gain': gain(ks[11], (N_ATTN_LAYERS, DA_HEADS, DA_VD)),
        'w_attn_out': nrm(ks[12], (N_ATTN_LAYERS, D_MODEL, D_MODEL), D_MODEL ** -0.5),
        'mlstm_norm': gain(ks[13], (N_MLSTM_LAYERS, D_MODEL)),
        'w_mlstm_in': nrm(ks[14], (N_MLSTM_LAYERS, D_MODEL, ML_IN), D_MODEL ** -0.5),
        'b_mlstm_gates': b_gates,
        'mlstm_head_gain': gain(ks[17], (N_MLSTM_LAYERS, ML_HEADS, ML_DV)),
        'w_mlstm_out': nrm(ks[18], (N_MLSTM_LAYERS, D_MODEL, D_MODEL), D_MODEL ** -0.5),
        'ffn_norm': gain(ks[19], (DEPTH, D_MODEL)),
        'w_ffn_gate': nrm(ks[20], (DEPTH, D_MODEL, FFN_HIDDEN), D_MODEL ** -0.5),
        'w_ffn_up': nrm(ks[21], (DEPTH, D_MODEL, FFN_HIDDEN), D_MODEL ** -0.5),
        'w_ffn_down': nrm(ks[22], (DEPTH, FFN_HIDDEN, D_MODEL), FFN_HIDDEN ** -0.5),
        'final_norm': gain(ks[23], (D_MODEL,)),
    }


def reference(x_prompt, x_sample, cache_k, cache_v, page_table, state_C, state_n, state_m,
              attn_norm, w_attn_qkv, attn_lambda, attn_head_gain, w_attn_out,
              mlstm_norm, w_mlstm_in, b_mlstm_gates, mlstm_head_gain, w_mlstm_out,
              ffn_norm, w_ffn_gate, w_ffn_up, w_ffn_down, final_norm):
    f32 = jnp.float32
    bp, tp = x_prompt.shape[:2]
    ts = x_sample.shape[1]
    past_len = page_table.shape[1] * PAGE_SIZE
    pos_p = jnp.arange(tp, dtype=f32)
    pos_s = past_len + jnp.arange(ts, dtype=f32)
    xp, xs = x_prompt, x_sample
    k_p, v_p, k_s, v_s = [], [], [], []
    c_p, n_p, m_p, c_s, n_s, m_s = [], [], [], [], [], []
    for i in range(DEPTH):
        if i % N_MIXERS == 0:
            a = i // N_MIXERS
            lam_init = 0.8 - 0.6 * math.exp(-0.3 * i)
            h = _rmsnorm(xp, attn_norm[a])
            q, k, v = _diff_attn_project(h, w_attn_qkv[a], pos_p)
            o = _prompt_two_map_attention(q, k, v)
            xp = xp + _diff_attn_output(o, attn_lambda[a], lam_init, attn_head_gain[a], w_attn_out[a])
            k_p.append(k.astype(cache_k.dtype))
            v_p.append(v.astype(cache_v.dtype))
            h = _rmsnorm(xs, attn_norm[a])
            q, k, v = _diff_attn_project(h, w_attn_qkv[a], pos_s)
            o = _paged_two_map_attention(q, k, v, cache_k[a], cache_v[a], page_table)
            xs = xs + _diff_attn_output(o, attn_lambda[a], lam_init, attn_head_gain[a], w_attn_out[a])
            k_s.append(k.astype(cache_k.dtype))
            v_s.append(v.astype(cache_v.dtype))
        else:
            j = i // N_MIXERS
            h = _rmsnorm(xp, mlstm_norm[j])
            q, k, v, og, ig, lf = _mlstm_project(h, w_mlstm_in[j], b_mlstm_gates[j])
            zc = jnp.zeros((bp, ML_HEADS, ML_DK, ML_DV), f32)
            zn = jnp.zeros((bp, ML_HEADS, ML_DK), f32)
            zm = jnp.zeros((bp, ML_HEADS), f32)
            hs, (C, n, m) = _mlstm_chunkwise(q, k, v, ig, lf, zc, zn, zm)
            xp = xp + _mlstm_output(hs, og, mlstm_head_gain[j], w_mlstm_out[j])
            c_p.append(C.astype(state_C.dtype))
            n_p.append(n.astype(state_n.dtype))
            m_p.append(m.astype(state_m.dtype))
            h = _rmsnorm(xs, mlstm_norm[j])
            q, k, v, og, ig, lf = _mlstm_project(h, w_mlstm_in[j], b_mlstm_gates[j])
            hs, (C, n, m) = _mlstm_chunkwise(q, k, v, ig, lf, state_C[j], state_n[j], state_m[j])
            xs = xs + _mlstm_output(hs, og, mlstm_head_gain[j], w_mlstm_out[j])
            c_s.append(C.astype(state_C.dtype))
            n_s.append(n.astype(state_n.dtype))
            m_s.append(m.astype(state_m.dtype))
        xp = xp + _swiglu(_rmsnorm(xp, ffn_norm[i]), w_ffn_gate[i], w_ffn_up[i], w_ffn_down[i])
        xs = xs + _swiglu(_rmsnorm(xs, ffn_norm[i]), w_ffn_gate[i], w_ffn_up[i], w_ffn_down[i])
    y_prompt = _rmsnorm(xp, final_norm)
    y_sample = _rmsnorm(xs, final_norm)
    return (y_prompt, y_sample,
            jnp.stack(k_p), jnp.stack(v_p), jnp.stack(c_p), jnp.stack(n_p), jnp.stack(m_p),
            jnp.stack(k_s), jnp.stack(v_s), jnp.stack(c_s), jnp.stack(n_s), jnp.stack(m_s))
```

```python
import functools
import math

import jax
import jax.numpy as jnp
from jax import lax
from jax.experimental import pallas as pl
from jax.experimental.pallas import tpu as pltpu

RMS_EPS = 1e-6
ROPE_THETA = 10000.0
GATE_SOFTCAP = 15.0
ML_CHUNK = 128
LANES = 128
NEG_BIG = -1e30
VMEM_LIMIT_BYTES = 56 * 1024 * 1024

F32 = jnp.float32
BF16 = jnp.bfloat16


def _params(*sem):
    return pltpu.CompilerParams(dimension_semantics=sem, vmem_limit_bytes=VMEM_LIMIT_BYTES)


def _resident(shape):
    nd = len(shape)
    return pl.BlockSpec(shape, lambda *_: (0,) * nd, pipeline_mode=pl.Buffered(1))


def _row_tile(n):
    return 512 if n % 512 == 0 else n


def _rms(x, gain):
    return x * lax.rsqrt(jnp.mean(x * x, axis=-1, keepdims=True) + RMS_EPS) * gain


def _nt_dot(a, b):
    return lax.dot_general(a, b, (((1,), (1,)), ((), ())), preferred_element_type=F32)


def _dot(a, b):
    return jnp.dot(a, b, preferred_element_type=F32)


def _rope_tables(pos):
    half = 32
    inv = ROPE_THETA ** (-jnp.arange(half, dtype=F32) * 2.0 / (2 * half))
    ang = pos[:, None] * inv[None, :]
    cos, sin = jnp.cos(ang), jnp.sin(ang)
    zero = jnp.zeros_like(sin)
    cos_t = jnp.tile(cos, (1, 4))
    sin_lo = jnp.tile(jnp.concatenate([-sin, zero], axis=1), (1, 2))
    sin_hi = jnp.tile(jnp.concatenate([zero, sin], axis=1), (1, 2))
    return cos_t, sin_lo, sin_hi


def _attn_proj_kernel(x_ref, g_ref, w_ref, cos_ref, slo_ref, shi_ref, q_ref, k_ref, v_ref, *, nqk, q_scale):
    h = _rms(x_ref[...], g_ref[...]).astype(BF16)
    cos, slo, shi = cos_ref[...], slo_ref[...], shi_ref[...]

    def rope_store(y, o_ref, scale):
        for c in range(nqk // LANES):
            yc = y[:, c * LANES:(c + 1) * LANES]
            r = yc * cos + pltpu.roll(yc, LANES - 32, 1) * slo + pltpu.roll(yc, 32, 1) * shi
            o_ref[:, c * LANES:(c + 1) * LANES] = r if scale == 1.0 else r * scale

    rope_store(_dot(h, w_ref[:, :nqk]), q_ref, q_scale)
    rope_store(_dot(h, w_ref[:, nqk:2 * nqk]), k_ref, 1.0)
    v_ref[...] = _dot(h, w_ref[:, 2 * nqk:])


def _attn_proj(x2d, gain, w_bf16, tables, nqk, nv, q_scale):
    n, d = x2d.shape
    tm = _row_tile(n)
    period = tables[0].shape[0] // tm
    row = lambda i: (i, 0)
    tab = pl.BlockSpec((tm, LANES), lambda i: (i % period, 0))
    return pl.pallas_call(
        functools.partial(_attn_proj_kernel, nqk=nqk, q_scale=q_scale),
        grid=(n // tm,),
        in_specs=[pl.BlockSpec((tm, d), row), _resident((1, d)), _resident(w_bf16.shape), tab, tab, tab],
        out_specs=[pl.BlockSpec((tm, nqk), row), pl.BlockSpec((tm, nqk), row), pl.BlockSpec((tm, nv), row)],
        out_shape=[jax.ShapeDtypeStruct((n, nqk), F32), jax.ShapeDtypeStruct((n, nqk), F32),
                   jax.ShapeDtypeStruct((n, nv), F32)],
        compiler_params=_params("parallel"),
        name="attn_proj",
    )(x2d, gain.reshape(1, d), w_bf16, *tables)


def _lambda(lam_ref, lam_init):
    lp = lam_ref[...]
    a = jnp.sum(lp[0:1, :] * lp[1:2, :], axis=-1, keepdims=True)
    b = jnp.sum(lp[2:3, :] * lp[3:4, :], axis=-1, keepdims=True)
    return jnp.exp(a) - jnp.exp(b) + lam_init


def _flash_kernel(lam_ref, gain_ref, q_ref, k_ref, v_ref, o_ref, *, tq, hd, lam_init):
    i = pl.program_id(2)
    q = q_ref[...].astype(BF16)
    lane = lax.broadcasted_iota(jnp.int32, q.shape, 1)
    zero = jnp.zeros_like(q)
    qq = jnp.concatenate([jnp.where(lane < hd, q, zero), jnp.where(lane >= hd, q, zero)], axis=0)

    def update(j, carry, diagonal):
        m, l, acc = carry
        rows = pl.ds(pl.multiple_of(j * tq, tq), tq)
        kc = k_ref[rows, :].astype(BF16)
        vc = v_ref[rows, :].astype(BF16)
        s = _nt_dot(qq, kc)
        if diagonal:
            r = lax.broadcasted_iota(jnp.int32, (2 * tq, tq), 0)
            c = lax.broadcasted_iota(jnp.int32, (2 * tq, tq), 1)
            s = jnp.where(c <= jnp.where(r >= tq, r - tq, r), s, -jnp.inf)
        m_new = jnp.maximum(m, jnp.max(s, axis=-1, keepdims=True))
        alpha = jnp.exp(m - m_new)
        p = jnp.exp(s - m_new)
        l = l * alpha + jnp.sum(p, axis=-1, keepdims=True)
        acc = acc * alpha + _dot(p.astype(BF16), vc)
        return m_new, l, acc

    init = (jnp.full((2 * tq, 1), -jnp.inf, F32), jnp.zeros((2 * tq, 1), F32),
            jnp.zeros((2 * tq, v_ref.shape[-1]), F32))
    carry = lax.fori_loop(0, i, lambda j, c: update(j, c, False), init)
    _, l, acc = update(i, carry, True)

    o = acc / l
    d = o[:tq] - _lambda(lam_ref, lam_init) * o[tq:]
    d = d * lax.rsqrt(jnp.mean(d * d, axis=-1, keepdims=True) + RMS_EPS)
    o_ref[...] = (d * gain_ref[0] * (1.0 - lam_init)).astype(o_ref.dtype)


def _flash_attention(q, k, v, lam_params, head_gain, batch, seq, lam_init):
    n, nqk = q.shape
    heads, vd = head_gain.shape
    hd = nqk // (2 * heads)
    tq = min(512, seq)
    nq = seq // tq
    return pl.pallas_call(
        functools.partial(_flash_kernel, tq=tq, hd=hd, lam_init=lam_init),
        grid=(batch, heads, nq),
        in_specs=[
            _resident(lam_params.shape),
            pl.BlockSpec((1, 1, vd), lambda b, h, i: (h, 0, 0)),
            pl.BlockSpec((tq, 2 * hd), lambda b, h, i: (b * nq + i, h)),
            pl.BlockSpec((seq, 2 * hd), lambda b, h, i: (b, h)),
            pl.BlockSpec((seq, vd), lambda b, h, i: (b, h)),
        ],
        out_specs=pl.BlockSpec((tq, vd), lambda b, h, i: (b * nq + i, h)),
        out_shape=jax.ShapeDtypeStruct((n, heads * vd), BF16),
        compiler_params=_params("parallel", "parallel", "arbitrary"),
        name="flash_attn",
    )(lam_params, head_gain.reshape(heads, 1, vd), q, k, v)


def _paged_kernel(pt_ref, lam_ref, gain_ref, qx_ref, kn_ref, vn_ref, *rest, pages, heads, tn, lam_init):
    k_refs, v_refs = rest[:pages], rest[pages:2 * pages]
    o_ref, m_sc, l_sc, acc_sc = rest[2 * pages:]
    j = pl.program_id(1)
    qx = qx_ref[0]
    rows = qx.shape[0]

    @pl.when(j == 0)
    def _():
        m_sc[...] = jnp.full(m_sc.shape, -jnp.inf, F32)
        l_sc[...] = jnp.zeros(l_sc.shape, F32)
        acc_sc[...] = jnp.zeros(acc_sc.shape, F32)

    def update(kp, vp, keep):
        s = _nt_dot(qx, kp.astype(BF16))
        if keep is not None:
            s = jnp.where(keep, s, -jnp.inf)
        m = m_sc[:, :1]
        m_new = jnp.maximum(m, jnp.max(s, axis=-1, keepdims=True))
        alpha = jnp.exp(m - m_new)
        p = jnp.exp(s - m_new)
        l_sc[...] = l_sc[...] * alpha + jnp.sum(p, axis=-1, keepdims=True)
        acc_sc[...] = acc_sc[...] * alpha + _dot(p.astype(BF16), vp.astype(BF16))
        m_sc[...] = jnp.broadcast_to(m_new, m_sc.shape)

    for r in range(pages):
        update(k_refs[r][...], v_refs[r][...], None)

    @pl.when(j == pl.num_programs(1) - 1)
    def _():
        keys = kn_ref.shape[1]
        t = lax.broadcasted_iota(jnp.int32, (rows, keys), 0) // (2 * heads)
        c = lax.broadcasted_iota(jnp.int32, (rows, keys), 1)
        update(kn_ref[0], vn_ref[0], c <= t)

        o = acc_sc[...] / l_sc[:, :1]
        lam = _lambda(lam_ref, lam_init)
        vd = o.shape[1] // heads
        own = (lax.broadcasted_iota(jnp.int32, (heads, o.shape[1]), 1) // vd
               == lax.broadcasted_iota(jnp.int32, (heads, o.shape[1]), 0))
        out = []
        for tok in range(tn):
            base = tok * 2 * heads
            d = o[base:base + heads] - lam * o[base + heads:base + 2 * heads]
            d = jnp.where(own, d, 0.0)
            d = d * lax.rsqrt(jnp.sum(d * d, axis=-1, keepdims=True) / vd + RMS_EPS)
            out.append(jnp.sum(d, axis=0, keepdims=True))
        o_ref[0] = jnp.concatenate(out, axis=0) * gain_ref[...] * (1.0 - lam_init)


def _paged_attention(q, k_new, v_new, cache_k, cache_v, layer, page_table, lam_params, head_gain, tn, lam_init):
    n, d = q.shape
    batch = n // tn
    heads, vd = head_gain.shape
    hd = d // (2 * heads)
    n_pages = page_table.shape[1]
    page = cache_k.shape[2]
    pages = 8 if n_pages % 8 == 0 else 1
    rows = tn * 2 * heads

    r = jnp.arange(rows)
    owner = 2 * (r % heads) + (r // heads) % 2
    lane_map = jnp.arange(d) // hd
    qx = jnp.where(owner[:, None] == lane_map[None, :],
                   jnp.repeat(q.reshape(batch, tn, d), 2 * heads, axis=1), 0.0).astype(BF16)

    def pad_keys(a):
        return jnp.pad(a.reshape(batch, tn, -1), ((0, 0), (0, page - tn), (0, 0)))

    ck = cache_k.reshape(cache_k.shape[0], cache_k.shape[1], page, d)
    cv = cache_v.reshape(cache_v.shape[0], cache_v.shape[1], page, heads * vd)

    def page_spec(width, slot):
        return pl.BlockSpec((None, None, page, width),
                            lambda b, j, pt: (layer, pt[b, j * pages + slot], 0, 0))

    per_b = lambda b, j, pt: (b, 0, 0)
    grid_spec = pltpu.PrefetchScalarGridSpec(
        num_scalar_prefetch=1,
        grid=(batch, n_pages // pages),
        in_specs=[
            pl.BlockSpec(lam_params.shape, lambda b, j, pt: (0, 0)),
            pl.BlockSpec((1, heads * vd), lambda b, j, pt: (0, 0)),
            pl.BlockSpec((1, rows, d), per_b),
            pl.BlockSpec((1, page, d), per_b),
            pl.BlockSpec((1, page, heads * vd), per_b),
        ] + [page_spec(d, s) for s in range(pages)] + [page_spec(heads * vd, s) for s in range(pages)],
        out_specs=pl.BlockSpec((1, tn, heads * vd), per_b),
        scratch_shapes=[pltpu.VMEM((rows, LANES), F32), pltpu.VMEM((rows, LANES), F32),
                        pltpu.VMEM((rows, heads * vd), F32)],
    )
    out = pl.pallas_call(
        functools.partial(_paged_kernel, pages=pages, heads=heads, tn=tn, lam_init=lam_init),
        grid_spec=grid_spec,
        out_shape=jax.ShapeDtypeStruct((batch, tn, heads * vd), F32),
        compiler_params=_params("parallel", "arbitrary"),
        name="paged_attn",
    )(page_table, lam_params, head_gain.reshape(1, heads * vd), qx, pad_keys(k_new), pad_keys(v_new),
      *([ck] * pages), *([cv] * pages))
    return out.reshape(n, heads * vd)


def _post_kernel(x_ref, d_ref, wo_ref, g_ref, wg_ref, wu_ref, wd_ref, fg_ref, o_ref, *, chunks, final):
    x1 = x_ref[...] + _dot(d_ref[...].astype(BF16), wo_ref[...])
    h = _rms(x1, g_ref[...]).astype(BF16)
    acc = x1
    for c0, c1 in chunks:
        g = _dot(h, wg_ref[:, c0:c1])
        u = _dot(h, wu_ref[:, c0:c1])
        a = (g / (1.0 + jnp.exp(-g)) * u).astype(BF16)
        acc = acc + _dot(a, wd_ref[c0:c1, :])
    o_ref[...] = _rms(acc, fg_ref[...]) if final else acc


def _post(x2d, d2d, wo, ffn_gain, wg, wu, wd, final_gain, final):
    n, dm = x2d.shape
    f = wg.shape[1]
    tm = _row_tile(n)
    step = 1024
    chunks = tuple((c, min(c + step, f)) for c in range(0, f, step))
    row = lambda i: (i, 0)
    return pl.pallas_call(
        functools.partial(_post_kernel, chunks=chunks, final=final),
        grid=(n // tm,),
        in_specs=[pl.BlockSpec((tm, dm), row), pl.BlockSpec((tm, dm), row), _resident(wo.shape),
                  _resident((1, dm)), _resident(wg.shape), _resident(wu.shape), _resident(wd.shape),
                  _resident((1, dm))],
        out_specs=pl.BlockSpec((tm, dm), row),
        out_shape=jax.ShapeDtypeStruct((n, dm), F32),
        compiler_params=_params("parallel"),
        name="post_final" if final else "post",
    )(x2d, d2d, wo, ffn_gain.reshape(1, dm), wg, wu, wd, final_gain.reshape(1, dm))


def _log_gates(g, is_forget):
    ig = GATE_SOFTCAP * jnp.tanh(g / GATE_SOFTCAP)
    lf = jnp.minimum(g, 0.0) - jnp.log1p(jnp.exp(-jnp.abs(g)))
    return jnp.where(is_forget, lf, ig)


def _mlstm_proj_kernel(x_ref, g_ref, w_ref, wgt_ref, bc_ref, br_ref,
                       q_ref, k_ref, v_ref, og_ref, gc_ref, gr_ref, *, hk, hv, q_scale):
    h = _rms(x_ref[...], g_ref[...]).astype(BF16)
    q_ref[...] = _dot(h, w_ref[:, :hk]) * q_scale
    k_ref[...] = _dot(h, w_ref[:, hk:2 * hk])
    v_ref[...] = _dot(h, w_ref[:, 2 * hk:2 * hk + hv])
    og_ref[...] = _dot(h, w_ref[:, 2 * hk + hv:])
    wgt = wgt_ref[...]
    ng = wgt.shape[0]
    gcol = lax.dot_general(h, wgt, (((1,), (1,)), ((), ())), preferred_element_type=F32) + bc_ref[...]
    grow = _nt_dot(wgt, h) + br_ref[...]
    gc_ref[...] = _log_gates(gcol, lax.broadcasted_iota(jnp.int32, gcol.shape, 1) >= ng // 2)
    gr_ref[...] = _log_gates(grow, lax.broadcasted_iota(jnp.int32, grow.shape, 0) >= ng // 2)


def _mlstm_proj(x2d, gain, w_main, w_gates_t, b_gates, hk, hv, q_scale):
    n, d = x2d.shape
    tm = _row_tile(n)
    ng = w_gates_t.shape[0]
    row = lambda i: (i, 0)
    widths = (hk, hk, hv, d)
    return pl.pallas_call(
        functools.partial(_mlstm_proj_kernel, hk=hk, hv=hv, q_scale=q_scale),
        grid=(n // tm,),
        in_specs=[pl.BlockSpec((tm, d), row), _resident((1, d)), _resident(w_main.shape),
                  _resident(w_gates_t.shape), _resident((1, ng)), _resident((ng, 1))],
        out_specs=[pl.BlockSpec((tm, w), row) for w in widths]
        + [pl.BlockSpec((tm, ng), row), pl.BlockSpec((ng, tm), lambda i: (0, i))],
        out_shape=[jax.ShapeDtypeStruct((n, w), F32) for w in widths]
        + [jax.ShapeDtypeStruct((n, ng), F32), jax.ShapeDtypeStruct((ng, n), F32)],
        compiler_params=_params("parallel"),
        name="mlstm_proj",
    )(x2d, gain.reshape(1, d), w_main, w_gates_t, b_gates.reshape(1, ng), b_gates.reshape(ng, 1))


def _split3(x):
    hi = x.astype(BF16)
    r1 = x - hi.astype(F32)
    mid = r1.astype(BF16)
    lo = (r1 - mid.astype(F32)).astype(BF16)
    return hi, mid, lo


def _mlstm_kernel(q_ref, k_ref, v_ref, og_ref, gc_ref, gr_ref, gain_ref, s0_ref, m0_ref,
                  y_ref, s_out_ref, m_out_ref, s_sc, m_sc, *, heads):
    pair = pl.program_id(1)
    c = pl.program_id(2)
    L, dk2 = q_ref.shape
    dk = dk2 // 2
    dv = v_ref.shape[1] // 2

    @pl.when(c == 0)
    def _():
        s_sc[...] = jnp.concatenate([s0_ref[0, 0], s0_ref[0, 1]], axis=0)
        m_sc[...] = m0_ref[0]

    tt = lax.broadcasted_iota(jnp.int32, (L, L), 0)
    ss = lax.broadcasted_iota(jnp.int32, (L, L), 1)
    causal = ss <= tt
    tril = jnp.where(causal, 1.0, 0.0).astype(BF16)
    triu = jnp.where(tt <= ss, 1.0, 0.0).astype(BF16)

    gc = gc_ref[...]
    gr = gr_ref[...]
    bcol = sum(_dot(tril, part) for part in _split3(gc))
    brow = sum(_dot(part, triu) for part in _split3(gr))
    col_id = lax.broadcasted_iota(jnp.int32, gc.shape, 1)

    row_id16 = lax.broadcasted_iota(jnp.int32, gr.shape, 0)

    def column(a, idx):
        return jnp.sum(jnp.where(col_id == idx, a, 0.0), axis=1, keepdims=True)

    def row(a, idx):
        return jnp.sum(jnp.where(row_id16 == idx, a, 0.0), axis=0, keepdims=True)

    q = q_ref[...].astype(BF16)
    k = k_ref[...]
    kt = k.T.astype(BF16)
    k = k.astype(BF16)
    lane = lax.broadcasted_iota(jnp.int32, (L, dk2), 1)
    sub = lax.broadcasted_iota(jnp.int32, (dk2, L), 0)
    state = s_sc[...]
    state_bf = state.astype(BF16)
    one_col = jnp.where(lax.broadcasted_iota(jnp.int32, (L, dv), 1) == 0, 1.0, 0.0)

    new_state = jnp.zeros_like(state)
    decay = []
    for hh in range(2):
        head = 2 * pair + hh
        i_col = column(gc, head)
        b_col = column(bcol, heads + head)
        i_row = row(gr, head)
        b_row = row(brow, heads + head)
        m_prev = m_sc[hh][:, :1]
        b_last = b_row[:, L - 1:L]

        log_d = jnp.where(causal, b_col - b_row + i_row, -jnp.inf)
        inter = b_col + m_prev
        mt = jnp.maximum(inter, jnp.max(log_d, axis=-1, keepdims=True))
        dm = jnp.exp(log_d - mt)
        wi = jnp.exp(inter - mt)

        mine = (lane >= hh * dk) & (lane < (hh + 1) * dk)
        qz = jnp.where(mine, q, jnp.zeros_like(q))
        sqk = _nt_dot(qz, k) * dm
        vx = jnp.concatenate([v_ref[:, hh * dv:(hh + 1) * dv], one_col], axis=1)
        nd = wi * _dot(qz, state_bf) + _dot(sqk.astype(BF16), vx.astype(BF16))
        den = nd[:, dv:dv + 1]
        hc = nd[:, :dv] / jnp.maximum(jnp.abs(den), jnp.exp(-mt))

        hn = hc * lax.rsqrt(jnp.mean(hc * hc, axis=-1, keepdims=True) + RMS_EPS) * gain_ref[hh]
        og = og_ref[:, hh * dv:(hh + 1) * dv]
        y_ref[:, hh * dv:(hh + 1) * dv] = (hn / (1.0 + jnp.exp(-og))).astype(y_ref.dtype)

        ls_row = b_last - b_row + i_row
        m_new = jnp.maximum(b_last + m_prev, jnp.max(ls_row, axis=-1, keepdims=True))
        ws_col = jnp.exp(b_last - b_col + i_col - m_new)
        wc = jnp.exp(b_last + m_prev - m_new)
        ktz = jnp.where((sub >= hh * dk) & (sub < (hh + 1) * dk), kt, jnp.zeros_like(kt))
        new_state = new_state + _dot(ktz, (ws_col * vx).astype(BF16))
        decay.append(wc)
        m_sc[hh] = jnp.broadcast_to(m_new, m_sc.shape[1:])

    row_id = lax.broadcasted_iota(jnp.int32, state.shape, 0)
    new_state = new_state + jnp.where(row_id < dk, decay[0], decay[1]) * state
    s_sc[...] = new_state

    @pl.when(c == pl.num_programs(2) - 1)
    def _():
        s_out_ref[0, 0] = new_state[:dk]
        s_out_ref[0, 1] = new_state[dk:]
        m_out_ref[0] = m_sc[...]


def _mlstm(q, k, v, og, gc, gr, head_gain, s0, m0, batch, nc):
    n, hk = q.shape
    heads, dv = head_gain.shape
    dk = hk // heads
    L = n // (batch * nc)
    ng = gc.shape[1]
    blk = lambda b, p, c: (b * nc + c, p)
    st = lambda b, p, c: (b, p, 0, 0)
    y, s_out, m_out = pl.pallas_call(
        functools.partial(_mlstm_kernel, heads=heads),
        grid=(batch, heads // 2, nc),
        in_specs=[
            pl.BlockSpec((L, 2 * dk), blk), pl.BlockSpec((L, 2 * dk), blk),
            pl.BlockSpec((L, 2 * dv), blk), pl.BlockSpec((L, 2 * dv), blk),
            pl.BlockSpec((L, ng), lambda b, p, c: (b * nc + c, 0)),
            pl.BlockSpec((ng, L), lambda b, p, c: (0, b * nc + c)),
            pl.BlockSpec((2, 1, dv), lambda b, p, c: (p, 0, 0)),
            pl.BlockSpec((1, 2, dk, 2 * dv), st),
            pl.BlockSpec((1, 2, 1, LANES), st),
        ],
        out_specs=[pl.BlockSpec((L, 2 * dv), blk), pl.BlockSpec((1, 2, dk, 2 * dv), st),
                   pl.BlockSpec((1, 2, 1, LANES), st)],
        out_shape=[jax.ShapeDtypeStruct((n, heads * dv), BF16),
                   jax.ShapeDtypeStruct((batch, heads, dk, 2 * dv), F32),
                   jax.ShapeDtypeStruct((batch, heads, 1, LANES), F32)],
        scratch_shapes=[pltpu.VMEM((2 * dk, 2 * dv), F32), pltpu.VMEM((2, 1, LANES), F32)],
        compiler_params=_params("parallel", "parallel", "arbitrary"),
        name="mlstm",
    )(q, k, v, og, gc, gr, head_gain.reshape(heads, 1, dv), s0, m0)
    return y, s_out[..., :dv], s_out[..., dv], m_out[:, :, 0, 0]


def _pack_state(C, nvec, m):
    dv = C.shape[-1]
    s0 = jnp.concatenate([C, nvec[..., None], jnp.zeros(C.shape[:-1] + (dv - 1,), F32)], axis=-1)
    return s0, jnp.broadcast_to(m[..., None, None], m.shape + (1, LANES))


def _pad_chunks(a, batch, t, L, value=0.0):
    w = a.shape[-1]
    return jnp.pad(a.reshape(batch, t, w), ((0, 0), (0, L - t), (0, 0)), constant_values=value).reshape(batch * L, w)


def _mlstm_layer(x2d, batch, t, norm_gain, w_main, w_gates_t, b_gates, head_gain, C0, n0, m0):
    heads, dv = head_gain.shape
    ng = 2 * heads
    hk = (w_main.shape[1] - heads * dv - x2d.shape[1]) // 2
    dk = hk // heads
    q, k, v, og, gc, gr = _mlstm_proj(x2d, norm_gain, w_main, w_gates_t, b_gates, hk, heads * dv, dk ** -0.5)
    s0, m0p = _pack_state(C0, n0, m0)
    if t % ML_CHUNK == 0:
        y, C, nvec, m = _mlstm(q, k, v, og, gc, gr, head_gain, s0, m0p, batch, t // ML_CHUNK)
        return y, C, nvec, m
    L = ML_CHUNK
    pad = lambda a: _pad_chunks(a, batch, t, L)
    gate_pad = jnp.concatenate([jnp.full((heads,), NEG_BIG, F32), jnp.zeros((heads,), F32)])
    gcp = jnp.where((jnp.arange(batch * L) % L < t)[:, None], pad(gc), gate_pad[None, :])
    grp = gcp.T
    y, C, nvec, m = _mlstm(pad(q), pad(k), pad(v), pad(og), gcp, grp, head_gain, s0, m0p, batch, 1)
    y = y.reshape(batch, L, -1)[:, :t].reshape(batch * t, -1)
    return y, C, nvec, m


def kernel(x_prompt, x_sample, cache_k, cache_v, page_table, state_C, state_n, state_m, attn_norm, w_attn_qkv, attn_lambda, attn_head_gain, w_attn_out, mlstm_norm, w_mlstm_in, b_mlstm_gates, mlstm_head_gain, w_mlstm_out, ffn_norm, w_ffn_gate, w_ffn_up, w_ffn_down, final_norm):
    bp, tp, dm = x_prompt.shape
    bs, ts, _ = x_sample.shape
    depth = ffn_norm.shape[0]
    n_mixers = 2
    da_heads, da_vd = attn_head_gain.shape[1:]
    ml_heads, ml_dv = mlstm_head_gain.shape[1:]
    nqk = (w_attn_qkv.shape[2] - da_heads * da_vd) // 2
    da_hd = nqk // (2 * da_heads)
    past_len = page_table.shape[1] * cache_k.shape[2]

    tab_p = _rope_tables(jnp.arange(tp, dtype=F32))
    tab_s = tuple(jnp.tile(a, (bs, 1)) for a in _rope_tables(past_len + jnp.arange(ts, dtype=F32)))

    xp = x_prompt.reshape(bp * tp, dm)
    xs = x_sample.reshape(bs * ts, dm)
    k_p, v_p, k_s, v_s = [], [], [], []
    c_p, n_p, m_p, c_s, n_s, m_s = [], [], [], [], [], []
    for i in range(depth):
        final = i == depth - 1
        if i % n_mixers == 0:
            a = i // n_mixers
            lam_init = 0.8 - 0.6 * math.exp(-0.3 * i)
            w_qkv = w_attn_qkv[a].astype(BF16)
            w_out = w_attn_out[a].astype(BF16)
            q, k, v = _attn_proj(xp, attn_norm[a], w_qkv, tab_p, nqk, da_heads * da_vd, da_hd ** -0.5)
            dp = _flash_attention(q, k, v, attn_lambda[a], attn_head_gain[a], bp, tp, lam_init)
            k_p.append(k.reshape(bp, tp, 2 * da_heads, da_hd))
            v_p.append(v.reshape(bp, tp, da_heads, da_vd))
            q, k, v = _attn_proj(xs, attn_norm[a], w_qkv, tab_s, nqk, da_heads * da_vd, da_hd ** -0.5)
            ds = _paged_attention(q, k, v, cache_k, cache_v, a, page_table, attn_lambda[a],
                                  attn_head_gain[a], ts, lam_init)
            k_s.append(k.reshape(bs, ts, 2 * da_heads, da_hd))
            v_s.append(v.reshape(bs, ts, da_heads, da_vd))
        else:
            j = i // n_mixers
            hv = ml_heads * ml_dv
            w_in = w_mlstm_in[j]
            w_main = w_in[:, :w_in.shape[1] - 2 * ml_heads].astype(BF16)
            w_gates_t = w_in[:, w_in.shape[1] - 2 * ml_heads:].T.astype(BF16)
            w_out = w_mlstm_out[j].astype(BF16)
            hk = (w_main.shape[1] - hv - dm) // 2
            dk = hk // ml_heads
            zero = (jnp.zeros((bp, ml_heads, dk, ml_dv), F32), jnp.zeros((bp, ml_heads, dk), F32),
                    jnp.zeros((bp, ml_heads), F32))
            dp, C, nv, m = _mlstm_layer(xp, bp, tp, mlstm_norm[j], w_main, w_gates_t, b_mlstm_gates[j],
                                        mlstm_head_gain[j], *zero)
            c_p.append(C); n_p.append(nv); m_p.append(m)
            ds, C, nv, m = _mlstm_layer(xs, bs, ts, mlstm_norm[j], w_main, w_gates_t, b_mlstm_gates[j],
                                        mlstm_head_gain[j], state_C[j], state_n[j], state_m[j])
            c_s.append(C); n_s.append(nv); m_s.append(m)
        wg, wu, wd = (w[i].astype(BF16) for w in (w_ffn_gate, w_ffn_up, w_ffn_down))
        xp = _post(xp, dp, w_out, ffn_norm[i], wg, wu, wd, final_norm, final)
        xs = _post(xs, ds, w_out, ffn_norm[i], wg, wu, wd, final_norm, final)
    return (xp.reshape(bp, tp, dm), xs.reshape(bs, ts, dm),
            jnp.stack(k_p), jnp.stack(v_p), jnp.stack(c_p), jnp.stack(n_p), jnp.stack(m_p),
            jnp.stack(k_s), jnp.stack(v_s), jnp.stack(c_s), jnp.stack(n_s), jnp.stack(m_s))
```

```python
import functools
import math

import jax
import jax.numpy as jnp
from jax import lax
from jax.experimental import pallas as pl
from jax.experimental.pallas import tpu as pltpu

RMS_EPS = 1e-6
ROPE_THETA = 10000.0
GATE_SOFTCAP = 15.0
ML_CHUNK = 128
LANES = 128
NEG_BIG = -1e30
VMEM_LIMIT_BYTES = 56 * 1024 * 1024

F32 = jnp.float32
BF16 = jnp.bfloat16


def _params(*sem):
    return pltpu.CompilerParams(dimension_semantics=sem, vmem_limit_bytes=VMEM_LIMIT_BYTES)


def _resident(shape):
    nd = len(shape)
    return pl.BlockSpec(shape, lambda *_: (0,) * nd, pipeline_mode=pl.Buffered(1))


def _row_tile(n):
    return 512 if n % 512 == 0 else n


def _rms(x, gain):
    return x * lax.rsqrt(jnp.mean(x * x, axis=-1, keepdims=True) + RMS_EPS) * gain


def _nt_dot(a, b):
    return lax.dot_general(a, b, (((1,), (1,)), ((), ())), preferred_element_type=F32)


def _dot(a, b):
    return jnp.dot(a, b, preferred_element_type=F32)


def _rope_tables(pos):
    half = 32
    inv = ROPE_THETA ** (-jnp.arange(half, dtype=F32) * 2.0 / (2 * half))
    ang = pos[:, None] * inv[None, :]
    cos, sin = jnp.cos(ang), jnp.sin(ang)
    zero = jnp.zeros_like(sin)
    cos_t = jnp.tile(cos, (1, 4))
    sin_lo = jnp.tile(jnp.concatenate([-sin, zero], axis=1), (1, 2))
    sin_hi = jnp.tile(jnp.concatenate([zero, sin], axis=1), (1, 2))
    return (cos_t, sin_lo, sin_hi), (cos.T, sin.T)


def _rope_lanes(y, o_ref, cos, slo, shi, scale):
    for c in range(y.shape[1] // LANES):
        yc = y[:, c * LANES:(c + 1) * LANES]
        r = yc * cos + pltpu.roll(yc, LANES - 32, 1) * slo + pltpu.roll(yc, 32, 1) * shi
        o_ref[:, c * LANES:(c + 1) * LANES] = r if scale == 1.0 else r * scale


def _attn_proj_kernel(x_ref, g_ref, w_ref, cos_ref, slo_ref, shi_ref, q_ref, k_ref, v_ref, *, nqk, q_scale):
    h = _rms(x_ref[...], g_ref[...]).astype(BF16)
    cos, slo, shi = cos_ref[...], slo_ref[...], shi_ref[...]
    _rope_lanes(_dot(h, w_ref[:, :nqk]), q_ref, cos, slo, shi, q_scale)
    _rope_lanes(_dot(h, w_ref[:, nqk:2 * nqk]), k_ref, cos, slo, shi, 1.0)
    v_ref[...] = _dot(h, w_ref[:, 2 * nqk:])


def _attn_proj_kt_kernel(x_ref, g_ref, w_ref, wkt_ref, cos_ref, slo_ref, shi_ref, cst_ref, snt_ref,
                         q_ref, kt_ref, v_ref, *, nqk, hd, q_scale):
    h = _rms(x_ref[...], g_ref[...]).astype(BF16)
    _rope_lanes(_dot(h, w_ref[:, :nqk]), q_ref, cos_ref[...], slo_ref[...], shi_ref[...], q_scale)
    kt = _nt_dot(wkt_ref[...], h)
    cs, sn = cst_ref[...], snt_ref[...]
    half = hd // 2
    for mp in range(nqk // hd):
        x1 = kt[mp * hd:mp * hd + half]
        x2 = kt[mp * hd + half:(mp + 1) * hd]
        kt_ref[mp * hd:mp * hd + half, :] = x1 * cs - x2 * sn
        kt_ref[mp * hd + half:(mp + 1) * hd, :] = x2 * cs + x1 * sn
    v_ref[...] = _dot(h, w_ref[:, 2 * nqk:])


def _attn_proj(x2d, gain, w_bf16, tables, nqk, nv, q_scale):
    n, d = x2d.shape
    tm = _row_tile(n)
    period = tables[0].shape[0] // tm
    row = lambda i: (i, 0)
    tab = pl.BlockSpec((tm, LANES), lambda i: (i % period, 0))
    return pl.pallas_call(
        functools.partial(_attn_proj_kernel, nqk=nqk, q_scale=q_scale),
        grid=(n // tm,),
        in_specs=[pl.BlockSpec((tm, d), row), _resident((1, d)), _resident(w_bf16.shape), tab, tab, tab],
        out_specs=[pl.BlockSpec((tm, nqk), row), pl.BlockSpec((tm, nqk), row), pl.BlockSpec((tm, nv), row)],
        out_shape=[jax.ShapeDtypeStruct((n, nqk), F32), jax.ShapeDtypeStruct((n, nqk), F32),
                   jax.ShapeDtypeStruct((n, nv), F32)],
        compiler_params=_params("parallel"),
        name="attn_proj",
    )(x2d, gain.reshape(1, d), w_bf16, *tables)


def _attn_proj_kt(x2d, gain, w_bf16, wkt_bf16, tables, tables_t, batch, nqk, hd, nv, q_scale):
    n, d = x2d.shape
    seq = n // batch
    tm = _row_tile(seq)
    period = seq // tm
    row = lambda i: (i, 0)
    tab = pl.BlockSpec((tm, LANES), lambda i: (i % period, 0))
    tab_t = pl.BlockSpec((hd // 2, tm), lambda i: (0, i % period))
    return pl.pallas_call(
        functools.partial(_attn_proj_kt_kernel, nqk=nqk, hd=hd, q_scale=q_scale),
        grid=(n // tm,),
        in_specs=[pl.BlockSpec((tm, d), row), _resident((1, d)), _resident(w_bf16.shape),
                  _resident(wkt_bf16.shape), tab, tab, tab, tab_t, tab_t],
        out_specs=[pl.BlockSpec((tm, nqk), row),
                   pl.BlockSpec((None, nqk, tm), lambda i: (i // period, 0, i % period)),
                   pl.BlockSpec((tm, nv), row)],
        out_shape=[jax.ShapeDtypeStruct((n, nqk), F32), jax.ShapeDtypeStruct((batch, nqk, seq), F32),
                   jax.ShapeDtypeStruct((n, nv), F32)],
        compiler_params=_params("parallel"),
        name="attn_proj_kt",
    )(x2d, gain.reshape(1, d), w_bf16, wkt_bf16, *tables, *tables_t)


def _lambda(lam_ref, lam_init):
    lp = lam_ref[...]
    a = jnp.sum(lp[0:1, :] * lp[1:2, :], axis=-1, keepdims=True)
    b = jnp.sum(lp[2:3, :] * lp[3:4, :], axis=-1, keepdims=True)
    return jnp.exp(a) - jnp.exp(b) + lam_init


def _flash_kernel(lam_ref, gain_ref, q_ref, kt_ref, v_ref, o_ref, k_sc, v_sc, *, tq, hd, lam_init):
    i = pl.program_id(2)

    @pl.when(i == 0)
    def _():
        for c in range(k_sc.shape[0]):
            k_sc[c] = kt_ref[:, c * tq:(c + 1) * tq].astype(BF16)
            v_sc[c] = v_ref[c * tq:(c + 1) * tq, :].astype(BF16)

    q = q_ref[...].astype(BF16)
    lane = lax.broadcasted_iota(jnp.int32, q.shape, 1)
    zero = jnp.zeros_like(q)
    qq = jnp.concatenate([jnp.where(lane < hd, q, zero), jnp.where(lane >= hd, q, zero)], axis=0)

    def update(j, carry, diagonal):
        m, l, acc = carry
        vc = v_sc[j]
        s = _dot(qq, k_sc[j])
        if diagonal:
            r = lax.broadcasted_iota(jnp.int32, (2 * tq, tq), 0)
            c = lax.broadcasted_iota(jnp.int32, (2 * tq, tq), 1)
            s = jnp.where(c <= jnp.where(r >= tq, r - tq, r), s, -jnp.inf)
        m_new = jnp.maximum(m, jnp.max(s, axis=-1, keepdims=True))
        alpha = jnp.exp(m - m_new)
        p = jnp.exp(s - m_new)
        l = l * alpha + jnp.sum(p, axis=-1, keepdims=True)
        acc = acc * alpha + _dot(p.astype(BF16), vc)
        return m_new, l, acc

    init = (jnp.full((2 * tq, 1), -jnp.inf, F32), jnp.zeros((2 * tq, 1), F32),
            jnp.zeros((2 * tq, v_ref.shape[-1]), F32))
    carry = lax.fori_loop(0, i // 2, lambda jj, c: update(2 * jj + 1, update(2 * jj, c, False), False), init)
    carry = lax.cond(i % 2 == 1, lambda c: update(i - 1, c, False), lambda c: c, carry)
    _, l, acc = update(i, carry, True)

    o = acc / l
    d = o[:tq] - _lambda(lam_ref, lam_init) * o[tq:]
    d = d * lax.rsqrt(jnp.mean(d * d, axis=-1, keepdims=True) + RMS_EPS)
    o_ref[...] = (d * gain_ref[0] * (1.0 - lam_init)).astype(o_ref.dtype)


def _flash_attention(q, kt, v, lam_params, head_gain, batch, seq, lam_init):
    n, nqk = q.shape
    heads, vd = head_gain.shape
    hd = nqk // (2 * heads)
    tq = min(512, seq)
    nq = seq // tq
    return pl.pallas_call(
        functools.partial(_flash_kernel, tq=tq, hd=hd, lam_init=lam_init),
        grid=(batch, heads, nq),
        in_specs=[
            _resident(lam_params.shape),
            pl.BlockSpec((1, 1, vd), lambda b, h, i: (h, 0, 0)),
            pl.BlockSpec((tq, 2 * hd), lambda b, h, i: (b * nq + i, h)),
            pl.BlockSpec((None, 2 * hd, seq), lambda b, h, i: (b, h, 0)),
            pl.BlockSpec((seq, vd), lambda b, h, i: (b, h)),
        ],
        out_specs=pl.BlockSpec((tq, vd), lambda b, h, i: (b * nq + i, h)),
        out_shape=jax.ShapeDtypeStruct((n, heads * vd), BF16),
        scratch_shapes=[pltpu.VMEM((nq, 2 * hd, tq), BF16), pltpu.VMEM((nq, tq, vd), BF16)],
        compiler_params=_params("parallel", "parallel", "arbitrary"),
        name="flash_attn",
    )(lam_params, head_gain.reshape(heads, 1, vd), q, kt, v)


def _paged_kernel(pt_ref, lam_ref, gain_ref, qx_ref, kn_ref, vn_ref, *rest, pages, heads, tn, lam_init):
    k_refs, v_refs = rest[:pages], rest[pages:2 * pages]
    o_ref, m_sc, l_sc, acc_sc = rest[2 * pages:]
    j = pl.program_id(1)
    qx = qx_ref[0]
    rows = qx.shape[0]
    rph = rows // heads
    keys = kn_ref.shape[2]

    @pl.when(j == 0)
    def _():
        m_sc[...] = jnp.full(m_sc.shape, -jnp.inf, F32)
        l_sc[...] = jnp.zeros(l_sc.shape, F32)
        acc_sc[...] = jnp.zeros(acc_sc.shape, F32)

    def update(kt_list, v_list, keep):
        ss = [_dot(qx, kt.astype(BF16)) for kt in kt_list]
        if keep is not None:
            ss = [jnp.where(keep, s, -jnp.inf) for s in ss]
        m = m_sc[:, :1]
        m_new = m
        for s in ss:
            m_new = jnp.maximum(m_new, jnp.max(s, axis=-1, keepdims=True))
        alpha = jnp.exp(m - m_new)
        l = l_sc[:, :1] * alpha
        acc = acc_sc[...] * alpha
        for s, v_ref in zip(ss, v_list):
            p = jnp.exp(s - m_new)
            l = l + jnp.sum(p, axis=-1, keepdims=True)
            pb = p.astype(BF16)
            acc = acc + jnp.concatenate(
                [_dot(pb[h * rph:(h + 1) * rph], v_ref[pl.ds(h, keys, stride=heads), :].astype(BF16))
                 for h in range(heads)], axis=0)
        m_sc[...] = jnp.broadcast_to(m_new, m_sc.shape)
        l_sc[...] = jnp.broadcast_to(l, l_sc.shape)
        acc_sc[...] = acc

    update([r[...] for r in k_refs], v_refs, None)

    @pl.when(j == pl.num_programs(1) - 1)
    def _():
        t = lax.broadcasted_iota(jnp.int32, (rows, keys), 0) % tn
        c = lax.broadcasted_iota(jnp.int32, (rows, keys), 1)
        update([kn_ref[0]], [vn_ref.at[0]], c <= t)

        o = (acc_sc[...] / l_sc[:, :1]).reshape(heads, rph, acc_sc.shape[1])
        d = o[:, :tn, :] - _lambda(lam_ref, lam_init) * o[:, tn:, :]
        d = d * lax.rsqrt(jnp.mean(d * d, axis=-1, keepdims=True) + RMS_EPS)
        o_ref[0] = d * gain_ref[...] * (1.0 - lam_init)


def _paged_attention(q, k_new, v_new, cache_k, cache_v, layer, page_table, lam_params, head_gain, tn, lam_init):
    n, d = q.shape
    batch = n // tn
    heads, vd = head_gain.shape
    hd = d // (2 * heads)
    n_pages = page_table.shape[1]
    page = cache_k.shape[2]
    pages = 8 if n_pages % 8 == 0 else 1
    rows = tn * 2 * heads

    r = jnp.arange(rows)
    owner = r // tn
    lane_map = jnp.arange(d) // hd
    qx = jnp.where(owner[None, :, None] == lane_map[None, None, :],
                   jnp.tile(q.reshape(batch, tn, d), (1, 2 * heads, 1)), 0.0).astype(BF16)

    kn = jnp.pad(k_new.reshape(batch, tn, d).transpose(0, 2, 1), ((0, 0), (0, 0), (0, page - tn)))
    vn = jnp.pad(v_new.reshape(batch, tn, heads, vd), ((0, 0), (0, page - tn), (0, 0), (0, 0)))
    vn = vn.reshape(batch, page * heads, vd)

    nl, n_pool = cache_k.shape[:2]
    ck = cache_k.transpose(0, 1, 3, 4, 2).reshape(nl, n_pool, d, page)
    cv = cache_v.reshape(nl, n_pool, page * heads, vd)

    def page_spec(block_rows, block_cols, slot):
        return pl.BlockSpec((None, None, block_rows, block_cols),
                            lambda b, j, pt: (layer, pt[b, j * pages + slot], 0, 0))

    per_b = lambda b, j, pt: (b, 0, 0)
    grid_spec = pltpu.PrefetchScalarGridSpec(
        num_scalar_prefetch=1,
        grid=(batch, n_pages // pages),
        in_specs=[
            pl.BlockSpec(lam_params.shape, lambda b, j, pt: (0, 0)),
            pl.BlockSpec((heads, 1, vd), lambda b, j, pt: (0, 0, 0)),
            pl.BlockSpec((1, rows, d), per_b),
            pl.BlockSpec((1, d, page), per_b),
            pl.BlockSpec((1, page * heads, vd), per_b),
        ] + [page_spec(d, page, s) for s in range(pages)]
        + [page_spec(page * heads, vd, s) for s in range(pages)],
        out_specs=pl.BlockSpec((1, heads, tn, vd), lambda b, j, pt: (b, 0, 0, 0)),
        scratch_shapes=[pltpu.VMEM((rows, LANES), F32), pltpu.VMEM((rows, LANES), F32),
                        pltpu.VMEM((rows, vd), F32)],
    )
    out = pl.pallas_call(
        functools.partial(_paged_kernel, pages=pages, heads=heads, tn=tn, lam_init=lam_init),
        grid_spec=grid_spec,
        out_shape=jax.ShapeDtypeStruct((batch, heads, tn, vd), F32),
        compiler_params=_params("parallel", "arbitrary"),
        name="paged_attn",
    )(page_table, lam_params, head_gain.reshape(heads, 1, vd), qx, kn, vn, *([ck] * pages), *([cv] * pages))
    return out.transpose(0, 2, 1, 3).reshape(n, heads * vd)


def _post_kernel(x_ref, d_ref, wo_ref, g_ref, wg_ref, wu_ref, wd_ref, fg_ref, o_ref, *, chunks, final):
    x1 = x_ref[...] + _dot(d_ref[...].astype(BF16), wo_ref[...])
    h = _rms(x1, g_ref[...]).astype(BF16)
    acc = x1
    for c0, c1 in chunks:
        g = _dot(h, wg_ref[:, c0:c1])
        u = _dot(h, wu_ref[:, c0:c1])
        a = (g / (1.0 + jnp.exp(-g)) * u).astype(BF16)
        acc = acc + _dot(a, wd_ref[c0:c1, :])
    o_ref[...] = _rms(acc, fg_ref[...]) if final else acc


def _post(x2d, d2d, wo, ffn_gain, wg, wu, wd, final_gain, final):
    n, dm = x2d.shape
    f = wg.shape[1]
    tm = _row_tile(n)
    step = 1024
    chunks = tuple((c, min(c + step, f)) for c in range(0, f, step))
    row = lambda i: (i, 0)
    return pl.pallas_call(
        functools.partial(_post_kernel, chunks=chunks, final=final),
        grid=(n // tm,),
        in_specs=[pl.BlockSpec((tm, dm), row), pl.BlockSpec((tm, dm), row), _resident(wo.shape),
                  _resident((1, dm)), _resident(wg.shape), _resident(wu.shape), _resident(wd.shape),
                  _resident((1, dm))],
        out_specs=pl.BlockSpec((tm, dm), row),
        out_shape=jax.ShapeDtypeStruct((n, dm), F32),
        compiler_params=_params("parallel"),
        name="post_final" if final else "post",
    )(x2d, d2d, wo, ffn_gain.reshape(1, dm), wg, wu, wd, final_gain.reshape(1, dm))


def _log_gates(g, is_forget):
    ig = GATE_SOFTCAP * jnp.tanh(g / GATE_SOFTCAP)
    lf = jnp.minimum(g, 0.0) - jnp.log1p(jnp.exp(-jnp.abs(g)))
    return jnp.where(is_forget, lf, ig)


def _mlstm_proj_kernel(x_ref, g_ref, w_ref, wgt_ref, bc_ref, br_ref,
                       q_ref, k_ref, v_ref, og_ref, gc_ref, gr_ref, *, hk, hv, q_scale):
    h = _rms(x_ref[...], g_ref[...]).astype(BF16)
    q_ref[...] = _dot(h, w_ref[:, :hk]) * q_scale
    k_ref[...] = _dot(h, w_ref[:, hk:2 * hk])
    v_ref[...] = _dot(h, w_ref[:, 2 * hk:2 * hk + hv])
    og_ref[...] = _dot(h, w_ref[:, 2 * hk + hv:])
    wgt = wgt_ref[...]
    ng = wgt.shape[0]
    gcol = lax.dot_general(h, wgt, (((1,), (1,)), ((), ())), preferred_element_type=F32) + bc_ref[...]
    grow = _nt_dot(wgt, h) + br_ref[...]
    gc_ref[...] = _log_gates(gcol, lax.broadcasted_iota(jnp.int32, gcol.shape, 1) >= ng // 2)
    gr_ref[...] = _log_gates(grow, lax.broadcasted_iota(jnp.int32, grow.shape, 0) >= ng // 2)


def _mlstm_proj(x2d, gain, w_main, w_gates_t, b_gates, hk, hv, q_scale):
    n, d = x2d.shape
    tm = _row_tile(n)
    ng = w_gates_t.shape[0]
    row = lambda i: (i, 0)
    widths = (hk, hk, hv, d)
    return pl.pallas_call(
        functools.partial(_mlstm_proj_kernel, hk=hk, hv=hv, q_scale=q_scale),
        grid=(n // tm,),
        in_specs=[pl.BlockSpec((tm, d), row), _resident((1, d)), _resident(w_main.shape),
                  _resident(w_gates_t.shape), _resident((1, ng)), _resident((ng, 1))],
        out_specs=[pl.BlockSpec((tm, w), row) for w in widths]
        + [pl.BlockSpec((tm, ng), row), pl.BlockSpec((ng, tm), lambda i: (0, i))],
        out_shape=[jax.ShapeDtypeStruct((n, w), F32) for w in widths]
        + [jax.ShapeDtypeStruct((n, ng), F32), jax.ShapeDtypeStruct((ng, n), F32)],
        compiler_params=_params("parallel"),
        name="mlstm_proj",
    )(x2d, gain.reshape(1, d), w_main, w_gates_t, b_gates.reshape(1, ng), b_gates.reshape(ng, 1))


def _split3(x):
    hi = x.astype(BF16)
    r1 = x - hi.astype(F32)
    mid = r1.astype(BF16)
    lo = (r1 - mid.astype(F32)).astype(BF16)
    return hi, mid, lo


def _mlstm_kernel(q_ref, k_ref, v_ref, og_ref, gc_ref, gr_ref, gain_ref, s0_ref, m0_ref,
                  y_ref, s_out_ref, m_out_ref, s_sc, m_sc, *, heads):
    pair = pl.program_id(1)
    c = pl.program_id(2)
    L, dk2 = q_ref.shape
    dk = dk2 // 2
    dv = v_ref.shape[1] // 2

    @pl.when(c == 0)
    def _():
        s_sc[...] = jnp.concatenate([s0_ref[0, 0], s0_ref[0, 1]], axis=0)
        m_sc[...] = m0_ref[0]

    tt = lax.broadcasted_iota(jnp.int32, (L, L), 0)
    ss = lax.broadcasted_iota(jnp.int32, (L, L), 1)
    causal = ss <= tt
    tril = jnp.where(causal, 1.0, 0.0).astype(BF16)
    triu = jnp.where(tt <= ss, 1.0, 0.0).astype(BF16)

    gc = gc_ref[...]
    gr = gr_ref[...]
    bcol = sum(_dot(tril, part) for part in _split3(gc))
    brow = sum(_dot(part, triu) for part in _split3(gr))
    col_id = lax.broadcasted_iota(jnp.int32, gc.shape, 1)

    row_id16 = lax.broadcasted_iota(jnp.int32, gr.shape, 0)

    def column(a, idx):
        return jnp.sum(jnp.where(col_id == idx, a, 0.0), axis=1, keepdims=True)

    def row(a, idx):
        return jnp.sum(jnp.where(row_id16 == idx, a, 0.0), axis=0, keepdims=True)

    q = q_ref[...].astype(BF16)
    k = k_ref[...]
    kt = k.T.astype(BF16)
    k = k.astype(BF16)
    lane = lax.broadcasted_iota(jnp.int32, (L, dk2), 1)
    sub = lax.broadcasted_iota(jnp.int32, (dk2, L), 0)
    state = s_sc[...]
    state_bf = state.astype(BF16)
    one_col = jnp.where(lax.broadcasted_iota(jnp.int32, (L, dv), 1) == 0, 1.0, 0.0)

    new_state = jnp.zeros_like(state)
    decay = []
    for hh in range(2):
        head = 2 * pair + hh
        i_col = column(gc, head)
        b_col = column(bcol, heads + head)
        i_row = row(gr, head)
        b_row = row(brow, heads + head)
        m_prev = m_sc[hh][:, :1]
        b_last = b_row[:, L - 1:L]

        log_d = jnp.where(causal, b_col - b_row + i_row, -jnp.inf)
        inter = b_col + m_prev
        mt = jnp.maximum(inter, jnp.max(log_d, axis=-1, keepdims=True))
        dm = jnp.exp(log_d - mt)
        wi = jnp.exp(inter - mt)

        mine = (lane >= hh * dk) & (lane < (hh + 1) * dk)
        qz = jnp.where(mine, q, jnp.zeros_like(q))
        sqk = _nt_dot(qz, k) * dm
        vx = jnp.concatenate([v_ref[:, hh * dv:(hh + 1) * dv], one_col], axis=1)
        nd = wi * _dot(qz, state_bf) + _dot(sqk.astype(BF16), vx.astype(BF16))
        den = nd[:, dv:dv + 1]
        hc = nd[:, :dv] / jnp.maximum(jnp.abs(den), jnp.exp(-mt))

        hn = hc * lax.rsqrt(jnp.mean(hc * hc, axis=-1, keepdims=True) + RMS_EPS) * gain_ref[hh]
        og = og_ref[:, hh * dv:(hh + 1) * dv]
        y_ref[:, hh * dv:(hh + 1) * dv] = (hn / (1.0 + jnp.exp(-og))).astype(y_ref.dtype)

        ls_row = b_last - b_row + i_row
        m_new = jnp.maximum(b_last + m_prev, jnp.max(ls_row, axis=-1, keepdims=True))
        ws_col = jnp.exp(b_last - b_col + i_col - m_new)
        wc = jnp.exp(b_last + m_prev - m_new)
        ktz = jnp.where((sub >= hh * dk) & (sub < (hh + 1) * dk), kt, jnp.zeros_like(kt))
        new_state = new_state + _dot(ktz, (ws_col * vx).astype(BF16))
        decay.append(wc)
        m_sc[hh] = jnp.broadcast_to(m_new, m_sc.shape[1:])

    row_id = lax.broadcasted_iota(jnp.int32, state.shape, 0)
    new_state = new_state + jnp.where(row_id < dk, decay[0], decay[1]) * state
    s_sc[...] = new_state

    @pl.when(c == pl.num_programs(2) - 1)
    def _():
        s_out_ref[0, 0] = new_state[:dk]
        s_out_ref[0, 1] = new_state[dk:]
        m_out_ref[0] = m_sc[...]


def _mlstm(q, k, v, og, gc, gr, head_gain, s0, m0, batch, nc):
    n, hk = q.shape
    heads, dv = head_gain.shape
    dk = hk // heads
    L = n // (batch * nc)
    ng = gc.shape[1]
    blk = lambda b, p, c: (b * nc + c, p)
    st = lambda b, p, c: (b, p, 0, 0)
    y, s_out, m_out = pl.pallas_call(
        functools.partial(_mlstm_kernel, heads=heads),
        grid=(batch, heads // 2, nc),
        in_specs=[
            pl.BlockSpec((L, 2 * dk), blk), pl.BlockSpec((L, 2 * dk), blk),
            pl.BlockSpec((L, 2 * dv), blk), pl.BlockSpec((L, 2 * dv), blk),
            pl.BlockSpec((L, ng), lambda b, p, c: (b * nc + c, 0)),
            pl.BlockSpec((ng, L), lambda b, p, c: (0, b * nc + c)),
            pl.BlockSpec((2, 1, dv), lambda b, p, c: (p, 0, 0)),
            pl.BlockSpec((1, 2, dk, 2 * dv), st),
            pl.BlockSpec((1, 2, 1, LANES), st),
        ],
        out_specs=[pl.BlockSpec((L, 2 * dv), blk), pl.BlockSpec((1, 2, dk, 2 * dv), st),
                   pl.BlockSpec((1, 2, 1, LANES), st)],
        out_shape=[jax.ShapeDtypeStruct((n, heads * dv), BF16),
                   jax.ShapeDtypeStruct((batch, heads, dk, 2 * dv), F32),
                   jax.ShapeDtypeStruct((batch, heads, 1, LANES), F32)],
        scratch_shapes=[pltpu.VMEM((2 * dk, 2 * dv), F32), pltpu.VMEM((2, 1, LANES), F32)],
        compiler_params=_params("parallel", "parallel", "arbitrary"),
        name="mlstm",
    )(q, k, v, og, gc, gr, head_gain.reshape(heads, 1, dv), s0, m0)
    return y, s_out[..., :dv], s_out[..., dv], m_out[:, :, 0, 0]


def _pack_state(C, nvec, m):
    dv = C.shape[-1]
    s0 = jnp.concatenate([C, nvec[..., None], jnp.zeros(C.shape[:-1] + (dv - 1,), F32)], axis=-1)
    return s0, jnp.broadcast_to(m[..., None, None], m.shape + (1, LANES))


def _pad_chunks(a, batch, t, L, value=0.0):
    w = a.shape[-1]
    return jnp.pad(a.reshape(batch, t, w), ((0, 0), (0, L - t), (0, 0)), constant_values=value).reshape(batch * L, w)


def _mlstm_layer(x2d, batch, t, norm_gain, w_main, w_gates_t, b_gates, head_gain, C0, n0, m0):
    heads, dv = head_gain.shape
    ng = 2 * heads
    hk = (w_main.shape[1] - heads * dv - x2d.shape[1]) // 2
    dk = hk // heads
    q, k, v, og, gc, gr = _mlstm_proj(x2d, norm_gain, w_main, w_gates_t, b_gates, hk, heads * dv, dk ** -0.5)
    s0, m0p = _pack_state(C0, n0, m0)
    if t % ML_CHUNK == 0:
        y, C, nvec, m = _mlstm(q, k, v, og, gc, gr, head_gain, s0, m0p, batch, t // ML_CHUNK)
        return y, C, nvec, m
    L = ML_CHUNK
    pad = lambda a: _pad_chunks(a, batch, t, L)
    gate_pad = jnp.concatenate([jnp.full((heads,), NEG_BIG, F32), jnp.zeros((heads,), F32)])
    gcp = jnp.where((jnp.arange(batch * L) % L < t)[:, None], pad(gc), gate_pad[None, :])
    grp = gcp.T
    y, C, nvec, m = _mlstm(pad(q), pad(k), pad(v), pad(og), gcp, grp, head_gain, s0, m0p, batch, 1)
    y = y.reshape(batch, L, -1)[:, :t].reshape(batch * t, -1)
    return y, C, nvec, m


def kernel(x_prompt, x_sample, cache_k, cache_v, page_table, state_C, state_n, state_m, attn_norm, w_attn_qkv, attn_lambda, attn_head_gain, w_attn_out, mlstm_norm, w_mlstm_in, b_mlstm_gates, mlstm_head_gain, w_mlstm_out, ffn_norm, w_ffn_gate, w_ffn_up, w_ffn_down, final_norm):
    bp, tp, dm = x_prompt.shape
    bs, ts, _ = x_sample.shape
    depth = ffn_norm.shape[0]
    n_mixers = 2
    da_heads, da_vd = attn_head_gain.shape[1:]
    ml_heads, ml_dv = mlstm_head_gain.shape[1:]
    nqk = (w_attn_qkv.shape[2] - da_heads * da_vd) // 2
    da_hd = nqk // (2 * da_heads)
    past_len = page_table.shape[1] * cache_k.shape[2]

    tab_p, tab_pt = _rope_tables(jnp.arange(tp, dtype=F32))
    tab_s = tuple(jnp.tile(a, (bs, 1)) for a in _rope_tables(past_len + jnp.arange(ts, dtype=F32))[0])

    xp = x_prompt.reshape(bp * tp, dm)
    xs = x_sample.reshape(bs * ts, dm)
    k_p, v_p, k_s, v_s = [], [], [], []
    c_p, n_p, m_p, c_s, n_s, m_s = [], [], [], [], [], []
    for i in range(depth):
        final = i == depth - 1
        if i % n_mixers == 0:
            a = i // n_mixers
            lam_init = 0.8 - 0.6 * math.exp(-0.3 * i)
            w_qkv = w_attn_qkv[a].astype(BF16)
            w_out = w_attn_out[a].astype(BF16)
            wkt = w_attn_qkv[a][:, nqk:2 * nqk].T.astype(BF16)
            q, kt, v = _attn_proj_kt(xp, attn_norm[a], w_qkv, wkt, tab_p, tab_pt, bp, nqk, da_hd,
                                     da_heads * da_vd, da_hd ** -0.5)
            dp = _flash_attention(q, kt, v, attn_lambda[a], attn_head_gain[a], bp, tp, lam_init)
            k_p.append(kt.reshape(bp, 2 * da_heads, da_hd, tp).transpose(0, 3, 1, 2))
            v_p.append(v.reshape(bp, tp, da_heads, da_vd))
            q, k, v = _attn_proj(xs, attn_norm[a], w_qkv, tab_s, nqk, da_heads * da_vd, da_hd ** -0.5)
            ds = _paged_attention(q, k, v, cache_k, cache_v, a, page_table, attn_lambda[a],
                                  attn_head_gain[a], ts, lam_init)
            k_s.append(k.reshape(bs, ts, 2 * da_heads, da_hd))
            v_s.append(v.reshape(bs, ts, da_heads, da_vd))
        else:
            j = i // n_mixers
            hv = ml_heads * ml_dv
            w_in = w_mlstm_in[j]
            w_main = w_in[:, :w_in.shape[1] - 2 * ml_heads].astype(BF16)
            w_gates_t = w_in[:, w_in.shape[1] - 2 * ml_heads:].T.astype(BF16)
            w_out = w_mlstm_out[j].astype(BF16)
            hk = (w_main.shape[1] - hv - dm) // 2
            dk = hk // ml_heads
            zero = (jnp.zeros((bp, ml_heads, dk, ml_dv), F32), jnp.zeros((bp, ml_heads, dk), F32),
                    jnp.zeros((bp, ml_heads), F32))
            dp, C, nv, m = _mlstm_layer(xp, bp, tp, mlstm_norm[j], w_main, w_gates_t, b_mlstm_gates[j],
                                        mlstm_head_gain[j], *zero)
            c_p.append(C); n_p.append(nv); m_p.append(m)
            ds, C, nv, m = _mlstm_layer(xs, bs, ts, mlstm_norm[j], w_main, w_gates_t, b_mlstm_gates[j],
                                        mlstm_head_gain[j], state_C[j], state_n[j], state_m[j])
            c_s.append(C); n_s.append(nv); m_s.append(m)
        wg, wu, wd = (w[i].astype(BF16) for w in (w_ffn_gate, w_ffn_up, w_ffn_down))
        xp = _post(xp, dp, w_out, ffn_norm[i], wg, wu, wd, final_norm, final)
        xs = _post(xs, ds, w_out, ffn_norm[i], wg, wu, wd, final_norm, final)
    return (xp.reshape(bp, tp, dm), xs.reshape(bs, ts, dm),
            jnp.stack(k_p), jnp.stack(v_p), jnp.stack(c_p), jnp.stack(n_p), jnp.stack(m_p),
            jnp.stack(k_s), jnp.stack(v_s), jnp.stack(c_s), jnp.stack(n_s), jnp.stack(m_s))
```

```python
import functools
import math

import jax
import jax.numpy as jnp
from jax import lax
from jax.experimental import pallas as pl
from jax.experimental.pallas import tpu as pltpu

RMS_EPS = 1e-6
ROPE_THETA = 10000.0
GATE_SOFTCAP = 15.0
ML_CHUNK = 128
LANES = 128
NEG_BIG = -1e30
VMEM_LIMIT_BYTES = 56 * 1024 * 1024

F32 = jnp.float32
BF16 = jnp.bfloat16


def _params(*sem):
    return pltpu.CompilerParams(dimension_semantics=sem, vmem_limit_bytes=VMEM_LIMIT_BYTES)


def _resident(shape):
    nd = len(shape)
    return pl.BlockSpec(shape, lambda *_: (0,) * nd, pipeline_mode=pl.Buffered(1))


def _row_tile(n):
    return 512 if n % 512 == 0 else n


def _rms(x, gain):
    return x * lax.rsqrt(jnp.mean(x * x, axis=-1, keepdims=True) + RMS_EPS) * gain


def _nt_dot(a, b):
    return lax.dot_general(a, b, (((1,), (1,)), ((), ())), preferred_element_type=F32)


def _dot(a, b):
    return jnp.dot(a, b, preferred_element_type=F32)


def _rope_tables(pos):
    half = 32
    inv = ROPE_THETA ** (-jnp.arange(half, dtype=F32) * 2.0 / (2 * half))
    ang = pos[:, None] * inv[None, :]
    cos, sin = jnp.cos(ang), jnp.sin(ang)
    zero = jnp.zeros_like(sin)
    cos_t = jnp.tile(cos, (1, 4))
    sin_lo = jnp.tile(jnp.concatenate([-sin, zero], axis=1), (1, 2))
    sin_hi = jnp.tile(jnp.concatenate([zero, sin], axis=1), (1, 2))
    return (cos_t, sin_lo, sin_hi), (cos.T, sin.T)


def _rope_lanes(y, o_ref, cos, slo, shi, scale):
    for c in range(y.shape[1] // LANES):
        yc = y[:, c * LANES:(c + 1) * LANES]
        r = yc * cos + pltpu.roll(yc, LANES - 32, 1) * slo + pltpu.roll(yc, 32, 1) * shi
        o_ref[:, c * LANES:(c + 1) * LANES] = r if scale == 1.0 else r * scale


def _attn_proj_kernel(x_ref, g_ref, w_ref, cos_ref, slo_ref, shi_ref, q_ref, k_ref, v_ref, *, nqk, q_scale):
    h = _rms(x_ref[...], g_ref[...]).astype(BF16)
    cos, slo, shi = cos_ref[...], slo_ref[...], shi_ref[...]
    _rope_lanes(_dot(h, w_ref[:, :nqk]), q_ref, cos, slo, shi, q_scale)
    _rope_lanes(_dot(h, w_ref[:, nqk:2 * nqk]), k_ref, cos, slo, shi, 1.0)
    v_ref[...] = _dot(h, w_ref[:, 2 * nqk:])


def _attn_proj_kt_kernel(x_ref, g_ref, w_ref, wkt_ref, cos_ref, slo_ref, shi_ref, cst_ref, snt_ref,
                         q_ref, kt_ref, v_ref, *, nqk, hd, q_scale):
    h = _rms(x_ref[...], g_ref[...]).astype(BF16)
    _rope_lanes(_dot(h, w_ref[:, :nqk]), q_ref, cos_ref[...], slo_ref[...], shi_ref[...], q_scale)
    kt = _nt_dot(wkt_ref[...], h)
    cs, sn = cst_ref[...], snt_ref[...]
    half = hd // 2
    for mp in range(nqk // hd):
        x1 = kt[mp * hd:mp * hd + half]
        x2 = kt[mp * hd + half:(mp + 1) * hd]
        kt_ref[mp * hd:mp * hd + half, :] = x1 * cs - x2 * sn
        kt_ref[mp * hd + half:(mp + 1) * hd, :] = x2 * cs + x1 * sn
    v_ref[...] = _dot(h, w_ref[:, 2 * nqk:])


def _attn_proj(x2d, gain, w_bf16, tables, nqk, nv, q_scale):
    n, d = x2d.shape
    tm = _row_tile(n)
    period = tables[0].shape[0] // tm
    row = lambda i: (i, 0)
    tab = pl.BlockSpec((tm, LANES), lambda i: (i % period, 0))
    return pl.pallas_call(
        functools.partial(_attn_proj_kernel, nqk=nqk, q_scale=q_scale),
        grid=(n // tm,),
        in_specs=[pl.BlockSpec((tm, d), row), _resident((1, d)), _resident(w_bf16.shape), tab, tab, tab],
        out_specs=[pl.BlockSpec((tm, nqk), row), pl.BlockSpec((tm, nqk), row), pl.BlockSpec((tm, nv), row)],
        out_shape=[jax.ShapeDtypeStruct((n, nqk), F32), jax.ShapeDtypeStruct((n, nqk), F32),
                   jax.ShapeDtypeStruct((n, nv), F32)],
        compiler_params=_params("parallel"),
        name="attn_proj",
    )(x2d, gain.reshape(1, d), w_bf16, *tables)


def _attn_proj_kt(x2d, gain, w_bf16, wkt_bf16, tables, tables_t, batch, nqk, hd, nv, q_scale):
    n, d = x2d.shape
    seq = n // batch
    tm = _row_tile(seq)
    period = seq // tm
    row = lambda i: (i, 0)
    tab = pl.BlockSpec((tm, LANES), lambda i: (i % period, 0))
    tab_t = pl.BlockSpec((hd // 2, tm), lambda i: (0, i % period))
    return pl.pallas_call(
        functools.partial(_attn_proj_kt_kernel, nqk=nqk, hd=hd, q_scale=q_scale),
        grid=(n // tm,),
        in_specs=[pl.BlockSpec((tm, d), row), _resident((1, d)), _resident(w_bf16.shape),
                  _resident(wkt_bf16.shape), tab, tab, tab, tab_t, tab_t],
        out_specs=[pl.BlockSpec((tm, nqk), row),
                   pl.BlockSpec((None, nqk, tm), lambda i: (i // period, 0, i % period)),
                   pl.BlockSpec((tm, nv), row)],
        out_shape=[jax.ShapeDtypeStruct((n, nqk), F32), jax.ShapeDtypeStruct((batch, nqk, seq), F32),
                   jax.ShapeDtypeStruct((n, nv), F32)],
        compiler_params=_params("parallel"),
        name="attn_proj_kt",
    )(x2d, gain.reshape(1, d), w_bf16, wkt_bf16, *tables, *tables_t)


def _lambda(lam_ref, lam_init):
    lp = lam_ref[...]
    a = jnp.sum(lp[0:1, :] * lp[1:2, :], axis=-1, keepdims=True)
    b = jnp.sum(lp[2:3, :] * lp[3:4, :], axis=-1, keepdims=True)
    return jnp.exp(a) - jnp.exp(b) + lam_init


def _flash_kernel(lam_ref, gain_ref, q_ref, kt_ref, v_ref, o_ref, k_sc, v_sc, qq_sc, m_sc, l_sc, acc_sc,
                  *, tq, hd, lam_init):
    i = pl.program_id(2)

    @pl.when(i == 0)
    def _():
        for c in range(k_sc.shape[0]):
            k_sc[c] = kt_ref[:, c * tq:(c + 1) * tq].astype(BF16)
            v_sc[c] = v_ref[c * tq:(c + 1) * tq, :].astype(BF16)

    q = q_ref[...].astype(BF16)
    lane = lax.broadcasted_iota(jnp.int32, q.shape, 1)
    zero = jnp.zeros_like(q)
    qq_sc[:tq] = jnp.where(lane < hd, q, zero)
    qq_sc[tq:] = jnp.where(lane >= hd, q, zero)
    m_sc[...] = jnp.full(m_sc.shape, -jnp.inf, F32)
    l_sc[...] = jnp.zeros(l_sc.shape, F32)
    acc_sc[...] = jnp.zeros(acc_sc.shape, F32)

    def update(j, diagonal):
        s = _dot(qq_sc[...], k_sc[j])
        if diagonal:
            r = lax.broadcasted_iota(jnp.int32, (2 * tq, tq), 0)
            c = lax.broadcasted_iota(jnp.int32, (2 * tq, tq), 1)
            s = jnp.where(c <= jnp.where(r >= tq, r - tq, r), s, -jnp.inf)
        m_prev = m_sc[...]
        m_new = jnp.maximum(m_prev, jnp.max(s, axis=-1, keepdims=True))
        alpha = jnp.exp(m_prev - m_new)
        p = jnp.exp(s - jnp.concatenate([m_new] * (tq // LANES), axis=1))
        l_sc[...] = l_sc[...] * alpha + jnp.sum(p, axis=-1, keepdims=True)
        acc_sc[...] = acc_sc[...] * alpha + _dot(p.astype(BF16), v_sc[j])
        m_sc[...] = m_new

    def pair(jj, carry):
        update(2 * jj, False)
        update(2 * jj + 1, False)
        return carry

    lax.fori_loop(0, i // 2, pair, 0)

    @pl.when(i % 2 == 1)
    def _():
        update(i - 1, False)

    update(i, True)

    o = acc_sc[...] / l_sc[...]
    d = o[:tq] - _lambda(lam_ref, lam_init) * o[tq:]
    d = d * lax.rsqrt(jnp.mean(d * d, axis=-1, keepdims=True) + RMS_EPS)
    o_ref[...] = (d * gain_ref[0] * (1.0 - lam_init)).astype(o_ref.dtype)


def _flash_attention(q, kt, v, lam_params, head_gain, batch, seq, lam_init):
    n, nqk = q.shape
    heads, vd = head_gain.shape
    hd = nqk // (2 * heads)
    tq = min(512, seq)
    nq = seq // tq
    assert vd == LANES and 2 * hd == LANES and tq % LANES == 0
    return pl.pallas_call(
        functools.partial(_flash_kernel, tq=tq, hd=hd, lam_init=lam_init),
        grid=(batch, heads, nq),
        in_specs=[
            _resident(lam_params.shape),
            pl.BlockSpec((1, 1, vd), lambda b, h, i: (h, 0, 0)),
            pl.BlockSpec((tq, 2 * hd), lambda b, h, i: (b * nq + i, h)),
            pl.BlockSpec((None, 2 * hd, seq), lambda b, h, i: (b, h, 0)),
            pl.BlockSpec((seq, vd), lambda b, h, i: (b, h)),
        ],
        out_specs=pl.BlockSpec((tq, vd), lambda b, h, i: (b * nq + i, h)),
        out_shape=jax.ShapeDtypeStruct((n, heads * vd), BF16),
        scratch_shapes=[pltpu.VMEM((nq, 2 * hd, tq), BF16), pltpu.VMEM((nq, tq, vd), BF16),
                        pltpu.VMEM((2 * tq, 2 * hd), BF16), pltpu.VMEM((2 * tq, LANES), F32),
                        pltpu.VMEM((2 * tq, LANES), F32), pltpu.VMEM((2 * tq, vd), F32)],
        compiler_params=_params("parallel", "parallel", "arbitrary"),
        name="flash_attn",
    )(lam_params, head_gain.reshape(heads, 1, vd), q, kt, v)


def _paged_kernel(pt_ref, lam_ref, gain_ref, qx_ref, kn_ref, vn_ref, *rest, pages, heads, tn, lam_init):
    k_refs, v_refs = rest[:pages], rest[pages:2 * pages]
    o_ref, m_sc, l_sc, acc_sc = rest[2 * pages:]
    j = pl.program_id(1)
    qx = qx_ref[0]
    rows = qx.shape[0]
    rph = rows // heads
    keys = kn_ref.shape[2]

    @pl.when(j == 0)
    def _():
        m_sc[...] = jnp.full(m_sc.shape, -jnp.inf, F32)
        l_sc[...] = jnp.zeros(l_sc.shape, F32)
        acc_sc[...] = jnp.zeros(acc_sc.shape, F32)

    def update(kt_list, v_list, keep):
        ss = [_dot(qx, kt.astype(BF16)) for kt in kt_list]
        if keep is not None:
            ss = [jnp.where(keep, s, -jnp.inf) for s in ss]
        m = m_sc[:, :1]
        m_new = m
        for s in ss:
            m_new = jnp.maximum(m_new, jnp.max(s, axis=-1, keepdims=True))
        alpha = jnp.exp(m - m_new)
        l = l_sc[:, :1] * alpha
        acc = acc_sc[...] * alpha
        for s, v_ref in zip(ss, v_list):
            p = jnp.exp(s - m_new)
            l = l + jnp.sum(p, axis=-1, keepdims=True)
            pb = p.astype(BF16)
            acc = acc + jnp.concatenate(
                [_dot(pb[h * rph:(h + 1) * rph], v_ref[pl.ds(h, keys, stride=heads), :].astype(BF16))
                 for h in range(heads)], axis=0)
        m_sc[...] = jnp.broadcast_to(m_new, m_sc.shape)
        l_sc[...] = jnp.broadcast_to(l, l_sc.shape)
        acc_sc[...] = acc

    update([r[...] for r in k_refs], v_refs, None)

    @pl.when(j == pl.num_programs(1) - 1)
    def _():
        t = lax.broadcasted_iota(jnp.int32, (rows, keys), 0) % tn
        c = lax.broadcasted_iota(jnp.int32, (rows, keys), 1)
        update([kn_ref[0]], [vn_ref.at[0]], c <= t)

        o = (acc_sc[...] / l_sc[:, :1]).reshape(heads, rph, acc_sc.shape[1])
        d = o[:, :tn, :] - _lambda(lam_ref, lam_init) * o[:, tn:, :]
        d = d * lax.rsqrt(jnp.mean(d * d, axis=-1, keepdims=True) + RMS_EPS)
        o_ref[0] = d * gain_ref[...] * (1.0 - lam_init)


def _paged_attention(q, k_new, v_new, cache_k, cache_v, layer, page_table, lam_params, head_gain, tn, lam_init):
    n, d = q.shape
    batch = n // tn
    heads, vd = head_gain.shape
    hd = d // (2 * heads)
    n_pages = page_table.shape[1]
    page = cache_k.shape[2]
    pages = 8 if n_pages % 8 == 0 else 1
    rows = tn * 2 * heads

    r = jnp.arange(rows)
    owner = r // tn
    lane_map = jnp.arange(d) // hd
    qx = jnp.where(owner[None, :, None] == lane_map[None, None, :],
                   jnp.tile(q.reshape(batch, tn, d), (1, 2 * heads, 1)), 0.0).astype(BF16)

    kn = jnp.pad(k_new.reshape(batch, tn, d).transpose(0, 2, 1), ((0, 0), (0, 0), (0, page - tn)))
    vn = jnp.pad(v_new.reshape(batch, tn, heads, vd), ((0, 0), (0, page - tn), (0, 0), (0, 0)))
    vn = vn.reshape(batch, page * heads, vd)

    nl, n_pool = cache_k.shape[:2]
    ck = cache_k.transpose(0, 1, 3, 4, 2).reshape(nl, n_pool, d, page)
    cv = cache_v.reshape(nl, n_pool, page * heads, vd)

    def page_spec(block_rows, block_cols, slot):
        return pl.BlockSpec((None, None, block_rows, block_cols),
                            lambda b, j, pt: (layer, pt[b, j * pages + slot], 0, 0))

    per_b = lambda b, j, pt: (b, 0, 0)
    grid_spec = pltpu.PrefetchScalarGridSpec(
        num_scalar_prefetch=1,
        grid=(batch, n_pages // pages),
        in_specs=[
            pl.BlockSpec(lam_params.shape, lambda b, j, pt: (0, 0)),
            pl.BlockSpec((heads, 1, vd), lambda b, j, pt: (0, 0, 0)),
            pl.BlockSpec((1, rows, d), per_b),
            pl.BlockSpec((1, d, page), per_b),
            pl.BlockSpec((1, page * heads, vd), per_b),
        ] + [page_spec(d, page, s) for s in range(pages)]
        + [page_spec(page * heads, vd, s) for s in range(pages)],
        out_specs=pl.BlockSpec((1, heads, tn, vd), lambda b, j, pt: (b, 0, 0, 0)),
        scratch_shapes=[pltpu.VMEM((rows, LANES), F32), pltpu.VMEM((rows, LANES), F32),
                        pltpu.VMEM((rows, vd), F32)],
    )
    out = pl.pallas_call(
        functools.partial(_paged_kernel, pages=pages, heads=heads, tn=tn, lam_init=lam_init),
        grid_spec=grid_spec,
        out_shape=jax.ShapeDtypeStruct((batch, heads, tn, vd), F32),
        compiler_params=_params("parallel", "arbitrary"),
        name="paged_attn",
    )(page_table, lam_params, head_gain.reshape(heads, 1, vd), qx, kn, vn, *([ck] * pages), *([cv] * pages))
    return out.transpose(0, 2, 1, 3).reshape(n, heads * vd)


def _post_kernel(x_ref, d_ref, wo_ref, g_ref, wg_ref, wu_ref, wd_ref, fg_ref, o_ref, *, chunks, final):
    x1 = x_ref[...] + _dot(d_ref[...].astype(BF16), wo_ref[...])
    h = _rms(x1, g_ref[...]).astype(BF16)
    acc = x1
    for c0, c1 in chunks:
        g = _dot(h, wg_ref[:, c0:c1])
        u = _dot(h, wu_ref[:, c0:c1])
        a = (g / (1.0 + jnp.exp(-g)) * u).astype(BF16)
        acc = acc + _dot(a, wd_ref[c0:c1, :])
    o_ref[...] = _rms(acc, fg_ref[...]) if final else acc


def _post(x2d, d2d, wo, ffn_gain, wg, wu, wd, final_gain, final):
    n, dm = x2d.shape
    f = wg.shape[1]
    tm = _row_tile(n)
    step = 1024
    chunks = tuple((c, min(c + step, f)) for c in range(0, f, step))
    row = lambda i: (i, 0)
    return pl.pallas_call(
        functools.partial(_post_kernel, chunks=chunks, final=final),
        grid=(n // tm,),
        in_specs=[pl.BlockSpec((tm, dm), row), pl.BlockSpec((tm, dm), row), _resident(wo.shape),
                  _resident((1, dm)), _resident(wg.shape), _resident(wu.shape), _resident(wd.shape),
                  _resident((1, dm))],
        out_specs=pl.BlockSpec((tm, dm), row),
        out_shape=jax.ShapeDtypeStruct((n, dm), F32),
        compiler_params=_params("parallel"),
        name="post_final" if final else "post",
    )(x2d, d2d, wo, ffn_gain.reshape(1, dm), wg, wu, wd, final_gain.reshape(1, dm))


def _log_gates(g, is_forget):
    ig = GATE_SOFTCAP * jnp.tanh(g / GATE_SOFTCAP)
    lf = jnp.minimum(g, 0.0) - jnp.log1p(jnp.exp(-jnp.abs(g)))
    return jnp.where(is_forget, lf, ig)


def _mlstm_proj_kernel(x_ref, g_ref, w_ref, wgt_ref, bc_ref, br_ref,
                       q_ref, k_ref, v_ref, og_ref, gc_ref, gr_ref, *, hk, hv, q_scale):
    h = _rms(x_ref[...], g_ref[...]).astype(BF16)
    q_ref[...] = _dot(h, w_ref[:, :hk]) * q_scale
    k_ref[...] = _dot(h, w_ref[:, hk:2 * hk])
    v_ref[...] = _dot(h, w_ref[:, 2 * hk:2 * hk + hv])
    og_ref[...] = _dot(h, w_ref[:, 2 * hk + hv:])
    wgt = wgt_ref[...]
    ng = wgt.shape[0]
    gcol = lax.dot_general(h, wgt, (((1,), (1,)), ((), ())), preferred_element_type=F32) + bc_ref[...]
    grow = _nt_dot(wgt, h) + br_ref[...]
    gc_ref[...] = _log_gates(gcol, lax.broadcasted_iota(jnp.int32, gcol.shape, 1) >= ng // 2)
    gr_ref[...] = _log_gates(grow, lax.broadcasted_iota(jnp.int32, grow.shape, 0) >= ng // 2)


def _mlstm_proj(x2d, gain, w_main, w_gates_t, b_gates, hk, hv, q_scale):
    n, d = x2d.shape
    tm = _row_tile(n)
    ng = w_gates_t.shape[0]
    row = lambda i: (i, 0)
    widths = (hk, hk, hv, d)
    return pl.pallas_call(
        functools.partial(_mlstm_proj_kernel, hk=hk, hv=hv, q_scale=q_scale),
        grid=(n // tm,),
        in_specs=[pl.BlockSpec((tm, d), row), _resident((1, d)), _resident(w_main.shape),
                  _resident(w_gates_t.shape), _resident((1, ng)), _resident((ng, 1))],
        out_specs=[pl.BlockSpec((tm, w), row) for w in widths]
        + [pl.BlockSpec((tm, ng), row), pl.BlockSpec((ng, tm), lambda i: (0, i))],
        out_shape=[jax.ShapeDtypeStruct((n, w), F32) for w in widths]
        + [jax.ShapeDtypeStruct((n, ng), F32), jax.ShapeDtypeStruct((ng, n), F32)],
        compiler_params=_params("parallel"),
        name="mlstm_proj",
    )(x2d, gain.reshape(1, d), w_main, w_gates_t, b_gates.reshape(1, ng), b_gates.reshape(ng, 1))


def _split3(x):
    hi = x.astype(BF16)
    r1 = x - hi.astype(F32)
    mid = r1.astype(BF16)
    lo = (r1 - mid.astype(F32)).astype(BF16)
    return hi, mid, lo


def _mlstm_kernel(q_ref, k_ref, v_ref, og_ref, gc_ref, gr_ref, gain_ref, s0_ref, m0_ref,
                  y_ref, s_out_ref, m_out_ref, s_sc, m_sc, *, heads):
    c = pl.program_id(1)
    L = q_ref.shape[0]
    dk = q_ref.shape[1] // heads
    dk2 = 2 * dk
    dv = v_ref.shape[1] // heads

    @pl.when(c == 0)
    def _():
        for pr in range(heads // 2):
            s_sc[pr] = jnp.concatenate([s0_ref[0, 2 * pr], s0_ref[0, 2 * pr + 1]], axis=0)
        m_sc[...] = m0_ref[0]

    tt = lax.broadcasted_iota(jnp.int32, (L, L), 0)
    ss = lax.broadcasted_iota(jnp.int32, (L, L), 1)
    causal = ss <= tt
    tril = jnp.where(causal, 1.0, 0.0).astype(BF16)
    triu = jnp.where(tt <= ss, 1.0, 0.0).astype(BF16)

    gc = gc_ref[...]
    gr = gr_ref[...]
    bcol = sum(_dot(tril, part) for part in _split3(gc))
    brow = sum(_dot(part, triu) for part in _split3(gr))
    col_id = lax.broadcasted_iota(jnp.int32, gc.shape, 1)

    row_id16 = lax.broadcasted_iota(jnp.int32, gr.shape, 0)

    def column(a, idx):
        return jnp.sum(jnp.where(col_id == idx, a, 0.0), axis=1, keepdims=True)

    def row(a, idx):
        return jnp.sum(jnp.where(row_id16 == idx, a, 0.0), axis=0, keepdims=True)

    lane = lax.broadcasted_iota(jnp.int32, (L, dk2), 1)
    sub = lax.broadcasted_iota(jnp.int32, (dk2, L), 0)
    one_col = jnp.where(lax.broadcasted_iota(jnp.int32, (L, dv), 1) == 0, 1.0, 0.0)
    last = c == pl.num_programs(1) - 1

    for pr in range(heads // 2):
        q = q_ref[:, pr * dk2:(pr + 1) * dk2].astype(BF16)
        k = k_ref[:, pr * dk2:(pr + 1) * dk2]
        kt = k.T.astype(BF16)
        k = k.astype(BF16)
        state = s_sc[pr]
        state_bf = state.astype(BF16)
        new_state = jnp.zeros_like(state)
        decay = []
        for hh in range(2):
            head = 2 * pr + hh
            i_col = column(gc, head)
            b_col = column(bcol, heads + head)
            i_row = row(gr, head)
            b_row = row(brow, heads + head)
            m_prev = m_sc[head][:, :1]
            b_last = b_row[:, L - 1:L]

            log_d = jnp.where(causal, b_col - b_row + i_row, -jnp.inf)
            inter = b_col + m_prev
            mt = jnp.maximum(inter, jnp.max(log_d, axis=-1, keepdims=True))
            dm = jnp.exp(log_d - mt)
            wi = jnp.exp(inter - mt)

            mine = (lane >= hh * dk) & (lane < (hh + 1) * dk)
            qz = jnp.where(mine, q, jnp.zeros_like(q))
            sqk = _nt_dot(qz, k) * dm
            vx = jnp.concatenate([v_ref[:, head * dv:(head + 1) * dv], one_col], axis=1)
            nd = wi * _dot(qz, state_bf) + _dot(sqk.astype(BF16), vx.astype(BF16))
            den = nd[:, dv:dv + 1]
            hc = nd[:, :dv] / jnp.maximum(jnp.abs(den), jnp.exp(-mt))

            hn = hc * lax.rsqrt(jnp.mean(hc * hc, axis=-1, keepdims=True) + RMS_EPS) * gain_ref[head]
            og = og_ref[:, head * dv:(head + 1) * dv]
            y_ref[:, head * dv:(head + 1) * dv] = (hn / (1.0 + jnp.exp(-og))).astype(y_ref.dtype)

            ls_row = b_last - b_row + i_row
            m_new = jnp.maximum(b_last + m_prev, jnp.max(ls_row, axis=-1, keepdims=True))
            ws_col = jnp.exp(b_last - b_col + i_col - m_new)
            wc = jnp.exp(b_last + m_prev - m_new)
            ktz = jnp.where((sub >= hh * dk) & (sub < (hh + 1) * dk), kt, jnp.zeros_like(kt))
            new_state = new_state + _dot(ktz, (ws_col * vx).astype(BF16))
            decay.append(wc)
            m_sc[head] = jnp.broadcast_to(m_new, m_sc.shape[1:])

        row_id = lax.broadcasted_iota(jnp.int32, state.shape, 0)
        new_state = new_state + jnp.where(row_id < dk, decay[0], decay[1]) * state
        s_sc[pr] = new_state

    @pl.when(last)
    def _():
        for pr in range(heads // 2):
            s_out_ref[0, 2 * pr] = s_sc[pr, :dk]
            s_out_ref[0, 2 * pr + 1] = s_sc[pr, dk:]
        m_out_ref[0] = m_sc[...]


def _mlstm(q, k, v, og, gc, gr, head_gain, s0, m0, batch, nc):
    n, hk = q.shape
    heads, dv = head_gain.shape
    dk = hk // heads
    L = n // (batch * nc)
    ng = gc.shape[1]
    blk = lambda b, c: (b * nc + c, 0)
    st = lambda b, c: (b, 0, 0, 0)
    y, s_out, m_out = pl.pallas_call(
        functools.partial(_mlstm_kernel, heads=heads),
        grid=(batch, nc),
        in_specs=[
            pl.BlockSpec((L, hk), blk), pl.BlockSpec((L, hk), blk),
            pl.BlockSpec((L, heads * dv), blk), pl.BlockSpec((L, heads * dv), blk),
            pl.BlockSpec((L, ng), blk),
            pl.BlockSpec((ng, L), lambda b, c: (0, b * nc + c)),
            pl.BlockSpec((heads, 1, dv), lambda b, c: (0, 0, 0)),
            pl.BlockSpec((1, heads, dk, 2 * dv), st),
            pl.BlockSpec((1, heads, 1, LANES), st),
        ],
        out_specs=[pl.BlockSpec((L, heads * dv), blk), pl.BlockSpec((1, heads, dk, 2 * dv), st),
                   pl.BlockSpec((1, heads, 1, LANES), st)],
        out_shape=[jax.ShapeDtypeStruct((n, heads * dv), BF16),
                   jax.ShapeDtypeStruct((batch, heads, dk, 2 * dv), F32),
                   jax.ShapeDtypeStruct((batch, heads, 1, LANES), F32)],
        scratch_shapes=[pltpu.VMEM((heads // 2, 2 * dk, 2 * dv), F32), pltpu.VMEM((heads, 1, LANES), F32)],
        compiler_params=_params("parallel", "arbitrary"),
        name="mlstm",
    )(q, k, v, og, gc, gr, head_gain.reshape(heads, 1, dv), s0, m0)
    return y, s_out[..., :dv], s_out[..., dv], m_out[:, :, 0, 0]


def _pack_state(C, nvec, m):
    dv = C.shape[-1]
    s0 = jnp.concatenate([C, nvec[..., None], jnp.zeros(C.shape[:-1] + (dv - 1,), F32)], axis=-1)
    return s0, jnp.broadcast_to(m[..., None, None], m.shape + (1, LANES))


def _pad_chunks(a, batch, t, L, value=0.0):
    w = a.shape[-1]
    return jnp.pad(a.reshape(batch, t, w), ((0, 0), (0, L - t), (0, 0)), constant_values=value).reshape(batch * L, w)


def _mlstm_layer(x2d, batch, t, norm_gain, w_main, w_gates_t, b_gates, head_gain, C0, n0, m0):
    heads, dv = head_gain.shape
    ng = 2 * heads
    hk = (w_main.shape[1] - heads * dv - x2d.shape[1]) // 2
    dk = hk // heads
    q, k, v, og, gc, gr = _mlstm_proj(x2d, norm_gain, w_main, w_gates_t, b_gates, hk, heads * dv, dk ** -0.5)
    s0, m0p = _pack_state(C0, n0, m0)
    if t % ML_CHUNK == 0:
        y, C, nvec, m = _mlstm(q, k, v, og, gc, gr, head_gain, s0, m0p, batch, t // ML_CHUNK)
        return y, C, nvec, m
    L = ML_CHUNK
    pad = lambda a: _pad_chunks(a, batch, t, L)
    gate_pad = jnp.concatenate([jnp.full((heads,), NEG_BIG, F32), jnp.zeros((heads,), F32)])
    gcp = jnp.where((jnp.arange(batch * L) % L < t)[:, None], pad(gc), gate_pad[None, :])
    grp = gcp.T
    y, C, nvec, m = _mlstm(pad(q), pad(k), pad(v), pad(og), gcp, grp, head_gain, s0, m0p, batch, 1)
    y = y.reshape(batch, L, -1)[:, :t].reshape(batch * t, -1)
    return y, C, nvec, m


def kernel(x_prompt, x_sample, cache_k, cache_v, page_table, state_C, state_n, state_m, attn_norm, w_attn_qkv, attn_lambda, attn_head_gain, w_attn_out, mlstm_norm, w_mlstm_in, b_mlstm_gates, mlstm_head_gain, w_mlstm_out, ffn_norm, w_ffn_gate, w_ffn_up, w_ffn_down, final_norm):
    bp, tp, dm = x_prompt.shape
    bs, ts, _ = x_sample.shape
    depth = ffn_norm.shape[0]
    n_mixers = 2
    da_heads, da_vd = attn_head_gain.shape[1:]
    ml_heads, ml_dv = mlstm_head_gain.shape[1:]
    nqk = (w_attn_qkv.shape[2] - da_heads * da_vd) // 2
    da_hd = nqk // (2 * da_heads)
    past_len = page_table.shape[1] * cache_k.shape[2]

    tab_p, tab_pt = _rope_tables(jnp.arange(tp, dtype=F32))
    tab_s = tuple(jnp.tile(a, (bs, 1)) for a in _rope_tables(past_len + jnp.arange(ts, dtype=F32))[0])

    xp = x_prompt.reshape(bp * tp, dm)
    xs = x_sample.reshape(bs * ts, dm)
    k_p, v_p, k_s, v_s = [], [], [], []
    c_p, n_p, m_p, c_s, n_s, m_s = [], [], [], [], [], []
    for i in range(depth):
        final = i == depth - 1
        if i % n_mixers == 0:
            a = i // n_mixers
            lam_init = 0.8 - 0.6 * math.exp(-0.3 * i)
            w_qkv = w_attn_qkv[a].astype(BF16)
            w_out = w_attn_out[a].astype(BF16)
            wkt = w_attn_qkv[a][:, nqk:2 * nqk].T.astype(BF16)
            q, kt, v = _attn_proj_kt(xp, attn_norm[a], w_qkv, wkt, tab_p, tab_pt, bp, nqk, da_hd,
                                     da_heads * da_vd, da_hd ** -0.5)
            dp = _flash_attention(q, kt, v, attn_lambda[a], attn_head_gain[a], bp, tp, lam_init)
            k_p.append(kt.reshape(bp, 2 * da_heads, da_hd, tp).transpose(0, 3, 1, 2))
            v_p.append(v.reshape(bp, tp, da_heads, da_vd))
            q, k, v = _attn_proj(xs, attn_norm[a], w_qkv, tab_s, nqk, da_heads * da_vd, da_hd ** -0.5)
            ds = _paged_attention(q, k, v, cache_k, cache_v, a, page_table, attn_lambda[a],
                                  attn_head_gain[a], ts, lam_init)
            k_s.append(k.reshape(bs, ts, 2 * da_heads, da_hd))
            v_s.append(v.reshape(bs, ts, da_heads, da_vd))
        else:
            j = i // n_mixers
            hv = ml_heads * ml_dv
            w_in = w_mlstm_in[j]
            w_main = w_in[:, :w_in.shape[1] - 2 * ml_heads].astype(BF16)
            w_gates_t = w_in[:, w_in.shape[1] - 2 * ml_heads:].T.astype(BF16)
            w_out = w_mlstm_out[j].astype(BF16)
            hk = (w_main.shape[1] - hv - dm) // 2
            dk = hk // ml_heads
            zero = (jnp.zeros((bp, ml_heads, dk, ml_dv), F32), jnp.zeros((bp, ml_heads, dk), F32),
                    jnp.zeros((bp, ml_heads), F32))
            dp, C, nv, m = _mlstm_layer(xp, bp, tp, mlstm_norm[j], w_main, w_gates_t, b_mlstm_gates[j],
                                        mlstm_head_gain[j], *zero)
            c_p.append(C); n_p.append(nv); m_p.append(m)
            ds, C, nv, m = _mlstm_layer(xs, bs, ts, mlstm_norm[j], w_main, w_gates_t, b_mlstm_gates[j],
                                        mlstm_head_gain[j], state_C[j], state_n[j], state_m[j])
            c_s.append(C); n_s.append(nv); m_s.append(m)
        wg, wu, wd = (w[i].astype(BF16) for w in (w_ffn_gate, w_ffn_up, w_ffn_down))
        xp = _post(xp, dp, w_out, ffn_norm[i], wg, wu, wd, final_norm, final)
        xs = _post(xs, ds, w_out, ffn_norm[i], wg, wu, wd, final_norm, final)
    return (xp.reshape(bp, tp, dm), xs.reshape(bs, ts, dm),
            jnp.stack(k_p), jnp.stack(v_p), jnp.stack(c_p), jnp.stack(n_p), jnp.stack(m_p),
            jnp.stack(k_s), jnp.stack(v_s), jnp.stack(c_s), jnp.stack(n_s), jnp.stack(m_s))
```

```python
import functools
import math

import jax
import jax.numpy as jnp
from jax import lax
from jax.experimental import pallas as pl
from jax.experimental.pallas import tpu as pltpu

RMS_EPS = 1e-6
ROPE_THETA = 10000.0
GATE_SOFTCAP = 15.0
ML_CHUNK = 128
SHORT_CHUNK = 16
LANES = 128
NEG_BIG = -1e30
VMEM_LIMIT_BYTES = 56 * 1024 * 1024

F32 = jnp.float32
BF16 = jnp.bfloat16


def _params(*sem):
    return pltpu.CompilerParams(dimension_semantics=sem, vmem_limit_bytes=VMEM_LIMIT_BYTES)


def _resident(shape):
    nd = len(shape)
    return pl.BlockSpec(shape, lambda *_: (0,) * nd, pipeline_mode=pl.Buffered(1))


def _row_tile(n):
    return 512 if n % 512 == 0 else n


def _rms(x, gain):
    return x * lax.rsqrt(jnp.mean(x * x, axis=-1, keepdims=True) + RMS_EPS) * gain


def _nt_dot(a, b):
    return lax.dot_general(a, b, (((1,), (1,)), ((), ())), preferred_element_type=F32)


def _dot(a, b):
    return jnp.dot(a, b, preferred_element_type=F32)


def _rope_tables(pos):
    half = 32
    inv = ROPE_THETA ** (-jnp.arange(half, dtype=F32) * 2.0 / (2 * half))
    ang = pos[:, None] * inv[None, :]
    cos, sin = jnp.cos(ang), jnp.sin(ang)
    zero = jnp.zeros_like(sin)
    cos_t = jnp.tile(cos, (1, 4))
    sin_lo = jnp.tile(jnp.concatenate([-sin, zero], axis=1), (1, 2))
    sin_hi = jnp.tile(jnp.concatenate([zero, sin], axis=1), (1, 2))
    return (cos_t, sin_lo, sin_hi), (cos.T, sin.T)


def _rope_lanes(y, o_ref, cos, slo, shi, scale):
    for c in range(y.shape[1] // LANES):
        yc = y[:, c * LANES:(c + 1) * LANES]
        r = yc * cos + pltpu.roll(yc, LANES - 32, 1) * slo + pltpu.roll(yc, 32, 1) * shi
        o_ref[:, c * LANES:(c + 1) * LANES] = r if scale == 1.0 else r * scale


def _attn_proj_kernel(x_ref, g_ref, w_ref, cos_ref, slo_ref, shi_ref, q_ref, k_ref, v_ref, *, nqk, q_scale):
    h = _rms(x_ref[...], g_ref[...]).astype(BF16)
    cos, slo, shi = cos_ref[...], slo_ref[...], shi_ref[...]
    _rope_lanes(_dot(h, w_ref[:, :nqk]), q_ref, cos, slo, shi, q_scale)
    _rope_lanes(_dot(h, w_ref[:, nqk:2 * nqk]), k_ref, cos, slo, shi, 1.0)
    v_ref[...] = _dot(h, w_ref[:, 2 * nqk:])


def _attn_proj_kt_kernel(x_ref, g_ref, w_ref, wkt_ref, cos_ref, slo_ref, shi_ref, cst_ref, snt_ref,
                         q_ref, kt_ref, v_ref, *, nqk, hd, q_scale):
    h = _rms(x_ref[...], g_ref[...]).astype(BF16)
    _rope_lanes(_dot(h, w_ref[:, :nqk]), q_ref, cos_ref[...], slo_ref[...], shi_ref[...], q_scale)
    kt = _nt_dot(wkt_ref[...], h)
    cs, sn = cst_ref[...], snt_ref[...]
    half = hd // 2
    for mp in range(nqk // hd):
        x1 = kt[mp * hd:mp * hd + half]
        x2 = kt[mp * hd + half:(mp + 1) * hd]
        kt_ref[mp * hd:mp * hd + half, :] = x1 * cs - x2 * sn
        kt_ref[mp * hd + half:(mp + 1) * hd, :] = x2 * cs + x1 * sn
    v_ref[...] = _dot(h, w_ref[:, 2 * nqk:])


def _attn_proj(x2d, gain, w_bf16, tables, nqk, nv, q_scale):
    n, d = x2d.shape
    tm = _row_tile(n)
    period = tables[0].shape[0] // tm
    row = lambda i: (i, 0)
    tab = pl.BlockSpec((tm, LANES), lambda i: (i % period, 0))
    return pl.pallas_call(
        functools.partial(_attn_proj_kernel, nqk=nqk, q_scale=q_scale),
        grid=(n // tm,),
        in_specs=[pl.BlockSpec((tm, d), row), _resident((1, d)), _resident(w_bf16.shape), tab, tab, tab],
        out_specs=[pl.BlockSpec((tm, nqk), row), pl.BlockSpec((tm, nqk), row), pl.BlockSpec((tm, nv), row)],
        out_shape=[jax.ShapeDtypeStruct((n, nqk), F32), jax.ShapeDtypeStruct((n, nqk), F32),
                   jax.ShapeDtypeStruct((n, nv), F32)],
        compiler_params=_params("parallel"),
        name="attn_proj",
    )(x2d, gain.reshape(1, d), w_bf16, *tables)


def _attn_proj_kt(x2d, gain, w_bf16, wkt_bf16, tables, tables_t, batch, nqk, hd, nv, q_scale):
    n, d = x2d.shape
    seq = n // batch
    tm = _row_tile(seq)
    period = seq // tm
    row = lambda i: (i, 0)
    tab = pl.BlockSpec((tm, LANES), lambda i: (i % period, 0))
    tab_t = pl.BlockSpec((hd // 2, tm), lambda i: (0, i % period))
    return pl.pallas_call(
        functools.partial(_attn_proj_kt_kernel, nqk=nqk, hd=hd, q_scale=q_scale),
        grid=(n // tm,),
        in_specs=[pl.BlockSpec((tm, d), row), _resident((1, d)), _resident(w_bf16.shape),
                  _resident(wkt_bf16.shape), tab, tab, tab, tab_t, tab_t],
        out_specs=[pl.BlockSpec((tm, nqk), row),
                   pl.BlockSpec((None, nqk, tm), lambda i: (i // period, 0, i % period)),
                   pl.BlockSpec((tm, nv), row)],
        out_shape=[jax.ShapeDtypeStruct((n, nqk), F32), jax.ShapeDtypeStruct((batch, nqk, seq), F32),
                   jax.ShapeDtypeStruct((n, nv), F32)],
        compiler_params=_params("parallel"),
        name="attn_proj_kt",
    )(x2d, gain.reshape(1, d), w_bf16, wkt_bf16, *tables, *tables_t)


def _lambda(lam_ref, lam_init):
    lp = lam_ref[...]
    a = jnp.sum(lp[0:1, :] * lp[1:2, :], axis=-1, keepdims=True)
    b = jnp.sum(lp[2:3, :] * lp[3:4, :], axis=-1, keepdims=True)
    return jnp.exp(a) - jnp.exp(b) + lam_init


def _flash_kernel(lam_ref, gain_ref, q_ref, kt_ref, v_ref, o_ref, k_sc, v_sc, qq_sc, m_sc, l_sc, acc_sc,
                  *, tq, hd, lam_init):
    i = pl.program_id(2)

    @pl.when(i == 0)
    def _():
        for c in range(k_sc.shape[0]):
            k_sc[c] = kt_ref[:, c * tq:(c + 1) * tq].astype(BF16)
            v_sc[c] = v_ref[c * tq:(c + 1) * tq, :].astype(BF16)

    q = q_ref[...].astype(BF16)
    lane = lax.broadcasted_iota(jnp.int32, q.shape, 1)
    zero = jnp.zeros_like(q)
    qq_sc[:tq] = jnp.where(lane < hd, q, zero)
    qq_sc[tq:] = jnp.where(lane >= hd, q, zero)
    m_sc[...] = jnp.full(m_sc.shape, -jnp.inf, F32)
    l_sc[...] = jnp.zeros(l_sc.shape, F32)
    acc_sc[...] = jnp.zeros(acc_sc.shape, F32)

    def update(j, diagonal):
        s = _dot(qq_sc[...], k_sc[j])
        if diagonal:
            r = lax.broadcasted_iota(jnp.int32, (2 * tq, tq), 0)
            c = lax.broadcasted_iota(jnp.int32, (2 * tq, tq), 1)
            s = jnp.where(c <= jnp.where(r >= tq, r - tq, r), s, -jnp.inf)
        m_prev = m_sc[...]
        m_new = jnp.maximum(m_prev, jnp.max(s, axis=-1, keepdims=True))
        alpha = jnp.exp(m_prev - m_new)
        p = jnp.exp(s - jnp.concatenate([m_new] * (tq // LANES), axis=1))
        l_sc[...] = l_sc[...] * alpha + jnp.sum(p, axis=-1, keepdims=True)
        acc_sc[...] = acc_sc[...] * alpha + _dot(p.astype(BF16), v_sc[j])
        m_sc[...] = m_new

    def pair(jj, carry):
        update(2 * jj, False)
        update(2 * jj + 1, False)
        return carry

    lax.fori_loop(0, i // 2, pair, 0)

    @pl.when(i % 2 == 1)
    def _():
        update(i - 1, False)

    update(i, True)

    o = acc_sc[...] / l_sc[...]
    d = o[:tq] - _lambda(lam_ref, lam_init) * o[tq:]
    d = d * lax.rsqrt(jnp.mean(d * d, axis=-1, keepdims=True) + RMS_EPS)
    o_ref[...] = (d * gain_ref[0] * (1.0 - lam_init)).astype(o_ref.dtype)


def _flash_attention(q, kt, v, lam_params, head_gain, batch, seq, lam_init):
    n, nqk = q.shape
    heads, vd = head_gain.shape
    hd = nqk // (2 * heads)
    tq = min(512, seq)
    nq = seq // tq
    assert vd == LANES and 2 * hd == LANES and tq % LANES == 0
    return pl.pallas_call(
        functools.partial(_flash_kernel, tq=tq, hd=hd, lam_init=lam_init),
        grid=(batch, heads, nq),
        in_specs=[
            _resident(lam_params.shape),
            pl.BlockSpec((1, 1, vd), lambda b, h, i: (h, 0, 0)),
            pl.BlockSpec((tq, 2 * hd), lambda b, h, i: (b * nq + i, h)),
            pl.BlockSpec((None, 2 * hd, seq), lambda b, h, i: (b, h, 0)),
            pl.BlockSpec((seq, vd), lambda b, h, i: (b, h)),
        ],
        out_specs=pl.BlockSpec((tq, vd), lambda b, h, i: (b * nq + i, h)),
        out_shape=jax.ShapeDtypeStruct((n, heads * vd), BF16),
        scratch_shapes=[pltpu.VMEM((nq, 2 * hd, tq), BF16), pltpu.VMEM((nq, tq, vd), BF16),
                        pltpu.VMEM((2 * tq, 2 * hd), BF16), pltpu.VMEM((2 * tq, LANES), F32),
                        pltpu.VMEM((2 * tq, LANES), F32), pltpu.VMEM((2 * tq, vd), F32)],
        compiler_params=_params("parallel", "parallel", "arbitrary"),
        name="flash_attn",
    )(lam_params, head_gain.reshape(heads, 1, vd), q, kt, v)


def _paged_kernel(pt_ref, lam_ref, gain_ref, qx_ref, kn_ref, vn_ref, *rest, pages, heads, tn, lam_init):
    k_refs, v_refs = rest[:pages], rest[pages:2 * pages]
    o_ref, m_sc, l_sc, acc_sc = rest[2 * pages:]
    j = pl.program_id(1)
    qx = qx_ref[0]
    rows = qx.shape[0]
    rph = rows // heads
    keys = kn_ref.shape[2]

    @pl.when(j == 0)
    def _():
        m_sc[...] = jnp.full(m_sc.shape, -jnp.inf, F32)
        l_sc[...] = jnp.zeros(l_sc.shape, F32)
        acc_sc[...] = jnp.zeros(acc_sc.shape, F32)

    def update(kt_list, v_list, keep):
        ss = [_dot(qx, kt.astype(BF16)) for kt in kt_list]
        if keep is not None:
            ss = [jnp.where(keep, s, -jnp.inf) for s in ss]
        m = m_sc[:, :1]
        m_new = m
        for s in ss:
            m_new = jnp.maximum(m_new, jnp.max(s, axis=-1, keepdims=True))
        alpha = jnp.exp(m - m_new)
        l = l_sc[:, :1] * alpha
        acc = acc_sc[...] * alpha
        for s, v_ref in zip(ss, v_list):
            p = jnp.exp(s - m_new)
            l = l + jnp.sum(p, axis=-1, keepdims=True)
            pb = p.astype(BF16)
            acc = acc + jnp.concatenate(
                [_dot(pb[h * rph:(h + 1) * rph], v_ref[pl.ds(h, s.shape[1], stride=heads), :].astype(BF16))
                 for h in range(heads)], axis=0)
        m_sc[...] = jnp.broadcast_to(m_new, m_sc.shape)
        l_sc[...] = jnp.broadcast_to(l, l_sc.shape)
        acc_sc[...] = acc

    update([r[...] for r in k_refs], v_refs, None)

    @pl.when(j == pl.num_programs(1) - 1)
    def _():
        t = lax.broadcasted_iota(jnp.int32, (rows, keys), 0) % tn
        c = lax.broadcasted_iota(jnp.int32, (rows, keys), 1)
        update([kn_ref[0]], [vn_ref.at[0]], c <= t)

        o = (acc_sc[...] / l_sc[:, :1]).reshape(heads, rph, acc_sc.shape[1])
        d = o[:, :tn, :] - _lambda(lam_ref, lam_init) * o[:, tn:, :]
        d = d * lax.rsqrt(jnp.mean(d * d, axis=-1, keepdims=True) + RMS_EPS)
        o_ref[0] = d * gain_ref[...] * (1.0 - lam_init)


def _paged_attention(q, k_new, v_new, cache_k, cache_v, layer, page_table, lam_params, head_gain, tn, lam_init):
    n, d = q.shape
    batch = n // tn
    heads, vd = head_gain.shape
    hd = d // (2 * heads)
    n_pages = page_table.shape[1]
    page = cache_k.shape[2]
    pages = 8 if n_pages % 8 == 0 else 1
    rows = tn * 2 * heads

    r = jnp.arange(rows)
    owner = r // tn
    lane_map = jnp.arange(d) // hd
    qx = jnp.where(owner[None, :, None] == lane_map[None, None, :],
                   jnp.tile(q.reshape(batch, tn, d), (1, 2 * heads, 1)), 0.0).astype(BF16)

    new_keys = -(-tn // SHORT_CHUNK) * SHORT_CHUNK
    kn = jnp.pad(k_new.reshape(batch, tn, d).transpose(0, 2, 1), ((0, 0), (0, 0), (0, new_keys - tn)))
    vn = jnp.pad(v_new.reshape(batch, tn, heads, vd), ((0, 0), (0, new_keys - tn), (0, 0), (0, 0)))
    vn = vn.reshape(batch, new_keys * heads, vd)

    nl, n_pool = cache_k.shape[:2]
    ck = cache_k.transpose(0, 1, 3, 4, 2).reshape(nl, n_pool, d, page)
    cv = cache_v.reshape(nl, n_pool, page * heads, vd)

    def page_spec(block_rows, block_cols, slot):
        return pl.BlockSpec((None, None, block_rows, block_cols),
                            lambda b, j, pt: (layer, pt[b, j * pages + slot], 0, 0))

    per_b = lambda b, j, pt: (b, 0, 0)
    grid_spec = pltpu.PrefetchScalarGridSpec(
        num_scalar_prefetch=1,
        grid=(batch, n_pages // pages),
        in_specs=[
            pl.BlockSpec(lam_params.shape, lambda b, j, pt: (0, 0)),
            pl.BlockSpec((heads, 1, vd), lambda b, j, pt: (0, 0, 0)),
            pl.BlockSpec((1, rows, d), per_b),
            pl.BlockSpec((1, d, new_keys), per_b),
            pl.BlockSpec((1, new_keys * heads, vd), per_b),
        ] + [page_spec(d, page, s) for s in range(pages)]
        + [page_spec(page * heads, vd, s) for s in range(pages)],
        out_specs=pl.BlockSpec((1, heads, tn, vd), lambda b, j, pt: (b, 0, 0, 0)),
        scratch_shapes=[pltpu.VMEM((rows, LANES), F32), pltpu.VMEM((rows, LANES), F32),
                        pltpu.VMEM((rows, vd), F32)],
    )
    out = pl.pallas_call(
        functools.partial(_paged_kernel, pages=pages, heads=heads, tn=tn, lam_init=lam_init),
        grid_spec=grid_spec,
        out_shape=jax.ShapeDtypeStruct((batch, heads, tn, vd), F32),
        compiler_params=_params("parallel", "arbitrary"),
        name="paged_attn",
    )(page_table, lam_params, head_gain.reshape(heads, 1, vd), qx, kn, vn, *([ck] * pages), *([cv] * pages))
    return out.transpose(0, 2, 1, 3).reshape(n, heads * vd)


def _post_kernel(x_ref, d_ref, wo_ref, g_ref, wg_ref, wu_ref, wd_ref, fg_ref, o_ref, *, chunks, final):
    x1 = x_ref[...] + _dot(d_ref[...].astype(BF16), wo_ref[...])
    h = _rms(x1, g_ref[...]).astype(BF16)
    acc = x1
    for c0, c1 in chunks:
        g = _dot(h, wg_ref[:, c0:c1])
        u = _dot(h, wu_ref[:, c0:c1])
        a = (g / (1.0 + jnp.exp(-g)) * u).astype(BF16)
        acc = acc + _dot(a, wd_ref[c0:c1, :])
    o_ref[...] = _rms(acc, fg_ref[...]) if final else acc


def _post(x2d, d2d, wo, ffn_gain, wg, wu, wd, final_gain, final):
    n, dm = x2d.shape
    f = wg.shape[1]
    tm = _row_tile(n)
    step = 1024
    chunks = tuple((c, min(c + step, f)) for c in range(0, f, step))
    row = lambda i: (i, 0)
    return pl.pallas_call(
        functools.partial(_post_kernel, chunks=chunks, final=final),
        grid=(n // tm,),
        in_specs=[pl.BlockSpec((tm, dm), row), pl.BlockSpec((tm, dm), row), _resident(wo.shape),
                  _resident((1, dm)), _resident(wg.shape), _resident(wu.shape), _resident(wd.shape),
                  _resident((1, dm))],
        out_specs=pl.BlockSpec((tm, dm), row),
        out_shape=jax.ShapeDtypeStruct((n, dm), F32),
        compiler_params=_params("parallel"),
        name="post_final" if final else "post",
    )(x2d, d2d, wo, ffn_gain.reshape(1, dm), wg, wu, wd, final_gain.reshape(1, dm))


def _log_gates(g, is_forget):
    ig = GATE_SOFTCAP * jnp.tanh(g / GATE_SOFTCAP)
    lf = jnp.minimum(g, 0.0) - jnp.log1p(jnp.exp(-jnp.abs(g)))
    return jnp.where(is_forget, lf, ig)


def _mlstm_proj_kernel(x_ref, g_ref, w_ref, wkt_ref, wgt_ref, bc_ref, br_ref,
                       q_ref, kt_ref, v_ref, og_ref, gc_ref, gr_ref, *, hk, hv, q_scale):
    h = _rms(x_ref[...], g_ref[...]).astype(BF16)
    q_ref[...] = _dot(h, w_ref[:, :hk]) * q_scale
    kt_ref[...] = _nt_dot(wkt_ref[...], h)
    v_ref[...] = _dot(h, w_ref[:, 2 * hk:2 * hk + hv])
    og_ref[...] = _dot(h, w_ref[:, 2 * hk + hv:])
    wgt = wgt_ref[...]
    ng = wgt.shape[0]
    gcol = lax.dot_general(h, wgt, (((1,), (1,)), ((), ())), preferred_element_type=F32) + bc_ref[...]
    grow = _nt_dot(wgt, h) + br_ref[...]
    gc_ref[...] = _log_gates(gcol, lax.broadcasted_iota(jnp.int32, gcol.shape, 1) >= ng // 2)
    gr_ref[...] = _log_gates(grow, lax.broadcasted_iota(jnp.int32, grow.shape, 0) >= ng // 2)


def _mlstm_proj(x2d, gain, w_main, w_k_t, w_gates_t, b_gates, hk, hv, q_scale):
    n, d = x2d.shape
    tm = _row_tile(n)
    ng = w_gates_t.shape[0]
    row = lambda i: (i, 0)
    col = lambda i: (0, i)
    return pl.pallas_call(
        functools.partial(_mlstm_proj_kernel, hk=hk, hv=hv, q_scale=q_scale),
        grid=(n // tm,),
        in_specs=[pl.BlockSpec((tm, d), row), _resident((1, d)), _resident(w_main.shape),
                  _resident(w_k_t.shape), _resident(w_gates_t.shape), _resident((1, ng)), _resident((ng, 1))],
        out_specs=[pl.BlockSpec((tm, hk), row), pl.BlockSpec((hk, tm), col), pl.BlockSpec((tm, hv), row),
                   pl.BlockSpec((tm, d), row), pl.BlockSpec((tm, ng), row), pl.BlockSpec((ng, tm), col)],
        out_shape=[jax.ShapeDtypeStruct((n, hk), F32), jax.ShapeDtypeStruct((hk, n), F32),
                   jax.ShapeDtypeStruct((n, hv), F32), jax.ShapeDtypeStruct((n, d), F32),
                   jax.ShapeDtypeStruct((n, ng), F32), jax.ShapeDtypeStruct((ng, n), F32)],
        compiler_params=_params("parallel"),
        name="mlstm_proj",
    )(x2d, gain.reshape(1, d), w_main, w_k_t, w_gates_t, b_gates.reshape(1, ng), b_gates.reshape(ng, 1))


def _split3(x):
    hi = x.astype(BF16)
    r1 = x - hi.astype(F32)
    mid = r1.astype(BF16)
    lo = (r1 - mid.astype(F32)).astype(BF16)
    return hi, mid, lo


def _mlstm_kernel(q_ref, kt_ref, v_ref, og_ref, gc_ref, gr_ref, gain_ref, s0_ref, m0_ref,
                  y_ref, s_out_ref, m_out_ref, s_sc, m_sc, *, heads):
    c = pl.program_id(1)
    L = q_ref.shape[0]
    dk = q_ref.shape[1] // heads
    dk2 = 2 * dk
    dv = v_ref.shape[1] // heads

    @pl.when(c == 0)
    def _():
        for pr in range(heads // 2):
            s_sc[pr] = jnp.concatenate([s0_ref[0, 2 * pr], s0_ref[0, 2 * pr + 1]], axis=0)
        m_sc[...] = m0_ref[0]

    tt = lax.broadcasted_iota(jnp.int32, (L, L), 0)
    ss = lax.broadcasted_iota(jnp.int32, (L, L), 1)
    causal = ss <= tt
    tril = jnp.where(causal, 1.0, 0.0).astype(BF16)
    triu = jnp.where(tt <= ss, 1.0, 0.0).astype(BF16)

    gc = gc_ref[...]
    gr = gr_ref[0]
    bcol = sum(_dot(tril, part) for part in _split3(gc))
    brow = sum(_dot(part, triu) for part in _split3(gr))
    lane = lax.broadcasted_iota(jnp.int32, (L, dk2), 1)
    sub = lax.broadcasted_iota(jnp.int32, (dk2, L), 0)
    one_col = jnp.where(lax.broadcasted_iota(jnp.int32, (L, dv), 1) == 0, 1.0, 0.0)
    last = c == pl.num_programs(1) - 1

    H = range(heads)
    P = range(heads // 2)
    qs = [q_ref[:, pr * dk2:(pr + 1) * dk2].astype(BF16) for pr in P]
    kts = [kt_ref[0, pr * dk2:(pr + 1) * dk2, :].astype(BF16) for pr in P]
    i_col = [gc[:, h:h + 1] for h in H]
    b_col = [bcol[:, heads + h:heads + h + 1] for h in H]
    i_row = [gr[h:h + 1, :] for h in H]
    b_row = [brow[heads + h:heads + h + 1, :] for h in H]
    b_last = [b[:, L - 1:L] for b in b_row]

    u = [jnp.where(causal, i_row[h] - b_row[h], -jnp.inf) for h in H]
    cm = [jnp.max(x, axis=-1, keepdims=True) for x in u]
    e = [jnp.exp(u[h] - cm[h]) for h in H]
    a = [b_col[h] + cm[h] for h in H]
    qz = [jnp.where((lane >= (h % 2) * dk) & (lane < (h % 2 + 1) * dk), qs[h // 2], jnp.zeros_like(qs[0]))
          for h in H]
    qk = [_dot(qz[h], kts[h // 2]) for h in H]
    g = [(qk[h] * e[h]).astype(BF16) for h in H]
    vx = [jnp.concatenate([v_ref[:, h * dv:(h + 1) * dv], one_col], axis=1) for h in H]
    intra = [_dot(g[h], vx[h].astype(BF16)) for h in H]

    m_loc = [jnp.max(b_last[h] - b_row[h] + i_row[h], axis=-1, keepdims=True) for h in H]
    ws_col = [jnp.exp(b_last[h] - b_col[h] + i_col[h] - m_loc[h]) for h in H]
    ktz = [jnp.where((sub >= (h % 2) * dk) & (sub < (h % 2 + 1) * dk), kts[h // 2], jnp.zeros_like(kts[0]))
           for h in H]
    upd = [_dot(ktz[h], (ws_col[h] * vx[h]).astype(BF16)) for h in H]

    state = [s_sc[pr] for pr in P]
    state_bf = [x.astype(BF16) for x in state]
    inter_c = [_dot(qz[h], state_bf[h // 2]) for h in H]
    m_prev = [m_sc[h][:, :1] for h in H]
    inter = [b_col[h] + m_prev[h] for h in H]
    mt = [jnp.maximum(inter[h], a[h]) for h in H]
    nd = [jnp.exp(inter[h] - mt[h]) * inter_c[h] + jnp.exp(a[h] - mt[h]) * intra[h] for h in H]
    inv = [1.0 / jnp.maximum(jnp.abs(nd[h][:, dv:dv + 1]), jnp.exp(-mt[h])) for h in H]
    msq = [jnp.mean(nd[h][:, :dv] * nd[h][:, :dv], axis=-1, keepdims=True) for h in H]
    scale = [inv[h] * lax.rsqrt(msq[h] * inv[h] * inv[h] + RMS_EPS) for h in H]
    sig = [1.0 + jnp.exp(-og_ref[:, h * dv:(h + 1) * dv]) for h in H]
    for h in H:
        y_ref[:, h * dv:(h + 1) * dv] = (nd[h][:, :dv] * scale[h] * gain_ref[h] / sig[h]).astype(y_ref.dtype)

    m_new = [jnp.maximum(b_last[h] + m_prev[h], m_loc[h]) for h in H]
    gain_new = [jnp.exp(m_loc[h] - m_new[h]) for h in H]
    decay = [jnp.exp(b_last[h] + m_prev[h] - m_new[h]) for h in H]
    row_id = lax.broadcasted_iota(jnp.int32, state[0].shape, 0)
    for pr in P:
        h0, h1 = 2 * pr, 2 * pr + 1
        s_sc[pr] = (gain_new[h0] * upd[h0] + gain_new[h1] * upd[h1]
                    + jnp.where(row_id < dk, decay[h0], decay[h1]) * state[pr])
    for h in H:
        m_sc[h] = jnp.broadcast_to(m_new[h], m_sc.shape[1:])

    @pl.when(last)
    def _():
        for pr in range(heads // 2):
            s_out_ref[0, 2 * pr] = s_sc[pr, :dk]
            s_out_ref[0, 2 * pr + 1] = s_sc[pr, dk:]
        m_out_ref[0] = m_sc[...]


def _mlstm(q, kt, v, og, gc, gr, head_gain, s0, m0, batch, nc):
    n, hk = q.shape
    heads, dv = head_gain.shape
    dk = hk // heads
    L = n // (batch * nc)
    ng = gc.shape[1]
    blk = lambda b, c: (b * nc + c, 0)
    st = lambda b, c: (b, 0, 0, 0)
    if kt.shape[0] == 1:
        lanes_blk = lambda b, c: (0, 0, b * nc + c)
    else:
        lanes_blk = lambda b, c: (b * nc + c, 0, 0)
    y, s_out, m_out = pl.pallas_call(
        functools.partial(_mlstm_kernel, heads=heads),
        grid=(batch, nc),
        in_specs=[
            pl.BlockSpec((L, hk), blk), pl.BlockSpec((1, hk, L), lanes_blk),
            pl.BlockSpec((L, heads * dv), blk), pl.BlockSpec((L, heads * dv), blk),
            pl.BlockSpec((L, ng), blk),
            pl.BlockSpec((1, ng, L), lanes_blk),
            pl.BlockSpec((heads, 1, dv), lambda b, c: (0, 0, 0)),
            pl.BlockSpec((1, heads, dk, 2 * dv), st),
            pl.BlockSpec((1, heads, 1, LANES), st),
        ],
        out_specs=[pl.BlockSpec((L, heads * dv), blk), pl.BlockSpec((1, heads, dk, 2 * dv), st),
                   pl.BlockSpec((1, heads, 1, LANES), st)],
        out_shape=[jax.ShapeDtypeStruct((n, heads * dv), BF16),
                   jax.ShapeDtypeStruct((batch, heads, dk, 2 * dv), F32),
                   jax.ShapeDtypeStruct((batch, heads, 1, LANES), F32)],
        scratch_shapes=[pltpu.VMEM((heads // 2, 2 * dk, 2 * dv), F32), pltpu.VMEM((heads, 1, LANES), F32)],
        compiler_params=_params("parallel", "arbitrary"),
        name="mlstm",
    )(q, kt, v, og, gc, gr, head_gain.reshape(heads, 1, dv), s0, m0)
    return y, s_out[..., :dv], s_out[..., dv], m_out[:, :, 0, 0]


def _pack_state(C, nvec, m):
    dv = C.shape[-1]
    s0 = jnp.concatenate([C, nvec[..., None], jnp.zeros(C.shape[:-1] + (dv - 1,), F32)], axis=-1)
    return s0, jnp.broadcast_to(m[..., None, None], m.shape + (1, LANES))


def _pad_chunks(a, batch, t, L, value=0.0):
    w = a.shape[-1]
    return jnp.pad(a.reshape(batch, t, w), ((0, 0), (0, L - t), (0, 0)), constant_values=value).reshape(batch * L, w)


def _mlstm_layer(x2d, batch, t, norm_gain, w_main, w_k_t, w_gates_t, b_gates, head_gain, C0, n0, m0):
    heads, dv = head_gain.shape
    ng = 2 * heads
    hk = w_k_t.shape[0]
    dk = hk // heads
    q, kt, v, og, gc, gr = _mlstm_proj(x2d, norm_gain, w_main, w_k_t, w_gates_t, b_gates, hk, heads * dv,
                                       dk ** -0.5)
    s0, m0p = _pack_state(C0, n0, m0)
    if t % ML_CHUNK == 0:
        y, C, nvec, m = _mlstm(q, kt[None], v, og, gc, gr[None], head_gain, s0, m0p, batch, t // ML_CHUNK)
        return y, C, nvec, m
    L = -(-t // SHORT_CHUNK) * SHORT_CHUNK
    pad = lambda a: _pad_chunks(a, batch, t, L)
    gate_pad = jnp.concatenate([jnp.full((heads,), NEG_BIG, F32), jnp.zeros((heads,), F32)])
    gcp = jnp.where((jnp.arange(batch * L) % L < t)[:, None], pad(gc), gate_pad[None, :])
    grp = gcp.reshape(batch, L, ng).transpose(0, 2, 1)
    ktp = jnp.pad(kt.reshape(hk, batch, t).transpose(1, 0, 2), ((0, 0), (0, 0), (0, L - t)))
    y, C, nvec, m = _mlstm(pad(q), ktp, pad(v), pad(og), gcp, grp, head_gain, s0, m0p, batch, 1)
    y = y.reshape(batch, L, -1)[:, :t].reshape(batch * t, -1)
    return y, C, nvec, m


def kernel(x_prompt, x_sample, cache_k, cache_v, page_table, state_C, state_n, state_m, attn_norm, w_attn_qkv, attn_lambda, attn_head_gain, w_attn_out, mlstm_norm, w_mlstm_in, b_mlstm_gates, mlstm_head_gain, w_mlstm_out, ffn_norm, w_ffn_gate, w_ffn_up, w_ffn_down, final_norm):
    bp, tp, dm = x_prompt.shape
    bs, ts, _ = x_sample.shape
    depth = ffn_norm.shape[0]
    n_mixers = 2
    da_heads, da_vd = attn_head_gain.shape[1:]
    ml_heads, ml_dv = mlstm_head_gain.shape[1:]
    nqk = (w_attn_qkv.shape[2] - da_heads * da_vd) // 2
    da_hd = nqk // (2 * da_heads)
    past_len = page_table.shape[1] * cache_k.shape[2]

    tab_p, tab_pt = _rope_tables(jnp.arange(tp, dtype=F32))
    tab_s = tuple(jnp.tile(a, (bs, 1)) for a in _rope_tables(past_len + jnp.arange(ts, dtype=F32))[0])

    xp = x_prompt.reshape(bp * tp, dm)
    xs = x_sample.reshape(bs * ts, dm)
    k_p, v_p, k_s, v_s = [], [], [], []
    c_p, n_p, m_p, c_s, n_s, m_s = [], [], [], [], [], []
    for i in range(depth):
        final = i == depth - 1
        if i % n_mixers == 0:
            a = i // n_mixers
            lam_init = 0.8 - 0.6 * math.exp(-0.3 * i)
            w_qkv = w_attn_qkv[a].astype(BF16)
            w_out = w_attn_out[a].astype(BF16)
            wkt = w_attn_qkv[a][:, nqk:2 * nqk].T.astype(BF16)
            q, kt, v = _attn_proj_kt(xp, attn_norm[a], w_qkv, wkt, tab_p, tab_pt, bp, nqk, da_hd,
                                     da_heads * da_vd, da_hd ** -0.5)
            dp = _flash_attention(q, kt, v, attn_lambda[a], attn_head_gain[a], bp, tp, lam_init)
            k_p.append(kt.reshape(bp, 2 * da_heads, da_hd, tp).transpose(0, 3, 1, 2))
            v_p.append(v.reshape(bp, tp, da_heads, da_vd))
            q, k, v = _attn_proj(xs, attn_norm[a], w_qkv, tab_s, nqk, da_heads * da_vd, da_hd ** -0.5)
            ds = _paged_attention(q, k, v, cache_k, cache_v, a, page_table, attn_lambda[a],
                                  attn_head_gain[a], ts, lam_init)
            k_s.append(k.reshape(bs, ts, 2 * da_heads, da_hd))
            v_s.append(v.reshape(bs, ts, da_heads, da_vd))
        else:
            j = i // n_mixers
            hv = ml_heads * ml_dv
            w_in = w_mlstm_in[j]
            w_main = w_in[:, :w_in.shape[1] - 2 * ml_heads].astype(BF16)
            w_gates_t = w_in[:, w_in.shape[1] - 2 * ml_heads:].T.astype(BF16)
            w_out = w_mlstm_out[j].astype(BF16)
            hk = (w_main.shape[1] - hv - dm) // 2
            dk = hk // ml_heads
            w_k_t = w_in[:, hk:2 * hk].T.astype(BF16)
            zero = (jnp.zeros((bp, ml_heads, dk, ml_dv), F32), jnp.zeros((bp, ml_heads, dk), F32),
                    jnp.zeros((bp, ml_heads), F32))
            dp, C, nv, m = _mlstm_layer(xp, bp, tp, mlstm_norm[j], w_main, w_k_t, w_gates_t, b_mlstm_gates[j],
                                        mlstm_head_gain[j], *zero)
            c_p.append(C); n_p.append(nv); m_p.append(m)
            ds, C, nv, m = _mlstm_layer(xs, bs, ts, mlstm_norm[j], w_main, w_k_t, w_gates_t, b_mlstm_gates[j],
                                        mlstm_head_gain[j], state_C[j], state_n[j], state_m[j])
            c_s.append(C); n_s.append(nv); m_s.append(m)
        wg, wu, wd = (w[i].astype(BF16) for w in (w_ffn_gate, w_ffn_up, w_ffn_down))
        xp = _post(xp, dp, w_out, ffn_norm[i], wg, wu, wd, final_norm, final)
        xs = _post(xs, ds, w_out, ffn_norm[i], wg, wu, wd, final_norm, final)
    return (xp.reshape(bp, tp, dm), xs.reshape(bs, ts, dm),
            jnp.stack(k_p), jnp.stack(v_p), jnp.stack(c_p), jnp.stack(n_p), jnp.stack(m_p),
            jnp.stack(k_s), jnp.stack(v_s), jnp.stack(c_s), jnp.stack(n_s), jnp.stack(m_s))
```

```python
import functools
import math

import jax
import jax.numpy as jnp
from jax import lax
from jax.experimental import pallas as pl
from jax.experimental.pallas import tpu as pltpu

RMS_EPS = 1e-6
ROPE_THETA = 10000.0
GATE_SOFTCAP = 15.0
ML_CHUNK = 128
SHORT_CHUNK = 16
LANES = 128
NEG_BIG = -1e30
LOG2E = math.log2(math.e)
VMEM_LIMIT_BYTES = 56 * 1024 * 1024

F32 = jnp.float32
BF16 = jnp.bfloat16


def _params(*sem):
    return pltpu.CompilerParams(dimension_semantics=sem, vmem_limit_bytes=VMEM_LIMIT_BYTES)


def _resident(shape):
    nd = len(shape)
    return pl.BlockSpec(shape, lambda *_: (0,) * nd, pipeline_mode=pl.Buffered(1))


def _row_tile(n):
    return 512 if n % 512 == 0 else n


def _rms(x, gain):
    return x * lax.rsqrt(jnp.mean(x * x, axis=-1, keepdims=True) + RMS_EPS) * gain


def _nt_dot(a, b):
    return lax.dot_general(a, b, (((1,), (1,)), ((), ())), preferred_element_type=F32)


def _dot(a, b):
    return jnp.dot(a, b, preferred_element_type=F32)


def _rope_tables(pos):
    half = 32
    inv = ROPE_THETA ** (-jnp.arange(half, dtype=F32) * 2.0 / (2 * half))
    ang = pos[:, None] * inv[None, :]
    cos, sin = jnp.cos(ang), jnp.sin(ang)
    zero = jnp.zeros_like(sin)
    cos_t = jnp.tile(cos, (1, 4))
    sin_lo = jnp.tile(jnp.concatenate([-sin, zero], axis=1), (1, 2))
    sin_hi = jnp.tile(jnp.concatenate([zero, sin], axis=1), (1, 2))
    return (cos_t, sin_lo, sin_hi), (cos.T, sin.T)


def _rope_lanes(y, o_ref, cos, slo, shi, scale):
    for c in range(y.shape[1] // LANES):
        yc = y[:, c * LANES:(c + 1) * LANES]
        r = yc * cos + pltpu.roll(yc, LANES - 32, 1) * slo + pltpu.roll(yc, 32, 1) * shi
        o_ref[:, c * LANES:(c + 1) * LANES] = r if scale == 1.0 else r * scale


def _attn_proj_kernel(x_ref, g_ref, w_ref, cos_ref, slo_ref, shi_ref, q_ref, k_ref, v_ref, *, nqk, q_scale):
    h = _rms(x_ref[...], g_ref[...]).astype(BF16)
    cos, slo, shi = cos_ref[...], slo_ref[...], shi_ref[...]
    _rope_lanes(_dot(h, w_ref[:, :nqk]), q_ref, cos, slo, shi, q_scale)
    _rope_lanes(_dot(h, w_ref[:, nqk:2 * nqk]), k_ref, cos, slo, shi, 1.0)
    v_ref[...] = _dot(h, w_ref[:, 2 * nqk:])


def _attn_proj_kt_kernel(x_ref, g_ref, w_ref, wkt_ref, cos_ref, slo_ref, shi_ref, cst_ref, snt_ref,
                         q_ref, kt_ref, v_ref, *, nqk, hd, q_scale):
    h = _rms(x_ref[...], g_ref[...]).astype(BF16)
    _rope_lanes(_dot(h, w_ref[:, :nqk]), q_ref, cos_ref[...], slo_ref[...], shi_ref[...], q_scale)
    kt = _nt_dot(wkt_ref[...], h)
    cs, sn = cst_ref[...], snt_ref[...]
    half = hd // 2
    for mp in range(nqk // hd):
        x1 = kt[mp * hd:mp * hd + half]
        x2 = kt[mp * hd + half:(mp + 1) * hd]
        kt_ref[mp * hd:mp * hd + half, :] = x1 * cs - x2 * sn
        kt_ref[mp * hd + half:(mp + 1) * hd, :] = x2 * cs + x1 * sn
    v_ref[...] = _dot(h, w_ref[:, 2 * nqk:])


def _attn_proj(x2d, gain, w_bf16, tables, nqk, nv, q_scale):
    n, d = x2d.shape
    tm = _row_tile(n)
    period = tables[0].shape[0] // tm
    row = lambda i: (i, 0)
    tab = pl.BlockSpec((tm, LANES), lambda i: (i % period, 0))
    return pl.pallas_call(
        functools.partial(_attn_proj_kernel, nqk=nqk, q_scale=q_scale),
        grid=(n // tm,),
        in_specs=[pl.BlockSpec((tm, d), row), _resident((1, d)), _resident(w_bf16.shape), tab, tab, tab],
        out_specs=[pl.BlockSpec((tm, nqk), row), pl.BlockSpec((tm, nqk), row), pl.BlockSpec((tm, nv), row)],
        out_shape=[jax.ShapeDtypeStruct((n, nqk), F32), jax.ShapeDtypeStruct((n, nqk), F32),
                   jax.ShapeDtypeStruct((n, nv), F32)],
        compiler_params=_params("parallel"),
        name="attn_proj",
    )(x2d, gain.reshape(1, d), w_bf16, *tables)


def _attn_proj_kt(x2d, gain, w_bf16, wkt_bf16, tables, tables_t, batch, nqk, hd, nv, q_scale):
    n, d = x2d.shape
    seq = n // batch
    tm = _row_tile(seq)
    period = seq // tm
    row = lambda i: (i, 0)
    tab = pl.BlockSpec((tm, LANES), lambda i: (i % period, 0))
    tab_t = pl.BlockSpec((hd // 2, tm), lambda i: (0, i % period))
    return pl.pallas_call(
        functools.partial(_attn_proj_kt_kernel, nqk=nqk, hd=hd, q_scale=q_scale),
        grid=(n // tm,),
        in_specs=[pl.BlockSpec((tm, d), row), _resident((1, d)), _resident(w_bf16.shape),
                  _resident(wkt_bf16.shape), tab, tab, tab, tab_t, tab_t],
        out_specs=[pl.BlockSpec((tm, nqk), row),
                   pl.BlockSpec((None, nqk, tm), lambda i: (i // period, 0, i % period)),
                   pl.BlockSpec((tm, nv), row)],
        out_shape=[jax.ShapeDtypeStruct((n, nqk), F32), jax.ShapeDtypeStruct((batch, nqk, seq), F32),
                   jax.ShapeDtypeStruct((n, nv), F32)],
        compiler_params=_params("parallel"),
        name="attn_proj_kt",
    )(x2d, gain.reshape(1, d), w_bf16, wkt_bf16, *tables, *tables_t)


def _lambda(lam_ref, lam_init):
    lp = lam_ref[...]
    a = jnp.sum(lp[0:1, :] * lp[1:2, :], axis=-1, keepdims=True)
    b = jnp.sum(lp[2:3, :] * lp[3:4, :], axis=-1, keepdims=True)
    return jnp.exp(a) - jnp.exp(b) + lam_init


def _flash_kernel(lam_ref, gain_ref, q_ref, kt_ref, v_ref, o_ref, k_sc, v_sc, qq_sc, m_sc, l_sc, acc_sc,
                  *, tq, hd, lam_init):
    i = pl.program_id(2)

    @pl.when(i == 0)
    def _():
        for c in range(k_sc.shape[0]):
            k_sc[c] = kt_ref[:, c * tq:(c + 1) * tq].astype(BF16)
            v_sc[c] = v_ref[c * tq:(c + 1) * tq, :].astype(BF16)

    q = q_ref[...].astype(BF16)
    lane = lax.broadcasted_iota(jnp.int32, q.shape, 1)
    zero = jnp.zeros_like(q)
    qq_sc[:tq] = jnp.where(lane < hd, q, zero)
    qq_sc[tq:] = jnp.where(lane >= hd, q, zero)
    m_sc[...] = jnp.full(m_sc.shape, -jnp.inf, F32)
    l_sc[...] = jnp.zeros(l_sc.shape, F32)
    acc_sc[...] = jnp.zeros(acc_sc.shape, F32)

    def update(j, diagonal):
        s = _dot(qq_sc[...], k_sc[j])
        if diagonal:
            r = lax.broadcasted_iota(jnp.int32, (2 * tq, tq), 0)
            c = lax.broadcasted_iota(jnp.int32, (2 * tq, tq), 1)
            s = jnp.where(c <= jnp.where(r >= tq, r - tq, r), s, -jnp.inf)
        m_prev = m_sc[...]
        m_new = jnp.maximum(m_prev, jnp.max(s, axis=-1, keepdims=True))
        alpha = jnp.exp2(m_prev - m_new)
        p = jnp.exp2(s - jnp.concatenate([m_new] * (tq // LANES), axis=1))
        acc_sc[...] = acc_sc[...] * alpha + _dot(p.astype(BF16), v_sc[j])
        l_sc[...] = l_sc[...] * alpha + jnp.sum(p, axis=-1, keepdims=True)
        m_sc[...] = m_new

    def pair(jj, carry):
        update(2 * jj, False)
        update(2 * jj + 1, False)
        return carry

    lax.fori_loop(0, i // 2, pair, 0)

    @pl.when(i % 2 == 1)
    def _():
        update(i - 1, False)

    update(i, True)

    o = acc_sc[...] / l_sc[...]
    d = o[:tq] - _lambda(lam_ref, lam_init) * o[tq:]
    d = d * lax.rsqrt(jnp.mean(d * d, axis=-1, keepdims=True) + RMS_EPS)
    o_ref[...] = (d * gain_ref[0] * (1.0 - lam_init)).astype(o_ref.dtype)


def _flash_attention(q, kt, v, lam_params, head_gain, batch, seq, lam_init):
    n, nqk = q.shape
    heads, vd = head_gain.shape
    hd = nqk // (2 * heads)
    tq = min(512, seq)
    nq = seq // tq
    assert vd == LANES and 2 * hd == LANES and tq % LANES == 0
    return pl.pallas_call(
        functools.partial(_flash_kernel, tq=tq, hd=hd, lam_init=lam_init),
        grid=(batch, heads, nq),
        in_specs=[
            _resident(lam_params.shape),
            pl.BlockSpec((1, 1, vd), lambda b, h, i: (h, 0, 0)),
            pl.BlockSpec((tq, 2 * hd), lambda b, h, i: (b * nq + i, h)),
            pl.BlockSpec((None, 2 * hd, seq), lambda b, h, i: (b, h, 0)),
            pl.BlockSpec((seq, vd), lambda b, h, i: (b, h)),
        ],
        out_specs=pl.BlockSpec((tq, vd), lambda b, h, i: (b * nq + i, h)),
        out_shape=jax.ShapeDtypeStruct((n, heads * vd), BF16),
        scratch_shapes=[pltpu.VMEM((nq, 2 * hd, tq), BF16), pltpu.VMEM((nq, tq, vd), BF16),
                        pltpu.VMEM((2 * tq, 2 * hd), BF16), pltpu.VMEM((2 * tq, LANES), F32),
                        pltpu.VMEM((2 * tq, LANES), F32), pltpu.VMEM((2 * tq, vd), F32)],
        compiler_params=_params("parallel", "parallel", "arbitrary"),
        name="flash_attn",
    )(lam_params, head_gain.reshape(heads, 1, vd), q, kt, v)


def _paged_kernel(pt_ref, lam_ref, gain_ref, qx_ref, kn_ref, vn_ref, *rest, pages, heads, tn, lam_init):
    k_refs, v_refs = rest[:pages], rest[pages:2 * pages]
    o_ref, m_sc, l_sc, acc_sc = rest[2 * pages:]
    j = pl.program_id(1)
    qx = qx_ref[0]
    rows = qx.shape[0]
    rph = rows // heads
    keys = kn_ref.shape[1]

    @pl.when(j == 0)
    def _():
        m_sc[...] = jnp.full(m_sc.shape, -jnp.inf, F32)
        l_sc[...] = jnp.zeros(l_sc.shape, F32)
        acc_sc[...] = jnp.zeros(acc_sc.shape, F32)

    def update(ss, v_list):
        m = m_sc[:, :1]
        m_new = m
        for s in ss:
            m_new = jnp.maximum(m_new, jnp.max(s, axis=-1, keepdims=True))
        alpha = jnp.exp2(m - m_new)
        l = l_sc[:, :1] * alpha
        acc = acc_sc[...] * alpha
        for s, v_ref in zip(ss, v_list):
            p = jnp.exp2(s - m_new)
            l = l + jnp.sum(p, axis=-1, keepdims=True)
            pb = p.astype(BF16)
            acc = acc + jnp.concatenate(
                [_dot(pb[h * rph:(h + 1) * rph], v_ref[pl.ds(h, s.shape[1], stride=heads), :].astype(BF16))
                 for h in range(heads)], axis=0)
        m_sc[...] = jnp.broadcast_to(m_new, m_sc.shape)
        l_sc[...] = jnp.broadcast_to(l, l_sc.shape)
        acc_sc[...] = acc

    group = 8
    for g0 in range(0, pages, group):
        update([_dot(qx, r[...].astype(BF16)) for r in k_refs[g0:g0 + group]], v_refs[g0:g0 + group])

    @pl.when(j == pl.num_programs(1) - 1)
    def _():
        t = lax.broadcasted_iota(jnp.int32, (rows, keys), 0) % tn
        c = lax.broadcasted_iota(jnp.int32, (rows, keys), 1)
        s_new = _nt_dot(qx, kn_ref[0].astype(BF16))
        update([jnp.where(c <= t, s_new, -jnp.inf)], [vn_ref.at[0]])

        o = (acc_sc[...] / l_sc[:, :1]).reshape(heads, rph, acc_sc.shape[1])
        d = o[:, :tn, :] - _lambda(lam_ref, lam_init) * o[:, tn:, :]
        d = d * lax.rsqrt(jnp.mean(d * d, axis=-1, keepdims=True) + RMS_EPS)
        o_ref[0] = d * gain_ref[...] * (1.0 - lam_init)


def _paged_attention(q, k_new, v_new, cache_k, cache_v, layer, page_table, lam_params, head_gain, tn, lam_init):
    n, d = q.shape
    batch = n // tn
    heads, vd = head_gain.shape
    hd = d // (2 * heads)
    n_pages = page_table.shape[1]
    page = cache_k.shape[2]
    pages = next(p for p in (16, 8, 4, 2, 1) if n_pages % p == 0)
    rows = tn * 2 * heads

    r = jnp.arange(rows)
    owner = r // tn
    lane_map = jnp.arange(d) // hd
    qx = jnp.where(owner[None, :, None] == lane_map[None, None, :],
                   jnp.tile(q.reshape(batch, tn, d), (1, 2 * heads, 1)), 0.0).astype(BF16)

    new_keys = -(-tn // SHORT_CHUNK) * SHORT_CHUNK
    kn = jnp.pad(k_new.reshape(batch, tn, d), ((0, 0), (0, new_keys - tn), (0, 0)))
    vn = jnp.pad(v_new.reshape(batch, tn, heads, vd), ((0, 0), (0, new_keys - tn), (0, 0), (0, 0)))
    vn = vn.reshape(batch, new_keys * heads, vd)

    nl, n_pool = cache_k.shape[:2]
    ck = cache_k.transpose(0, 1, 3, 4, 2).reshape(nl, n_pool, d, page)
    cv = cache_v.reshape(nl, n_pool, page * heads, vd)

    def page_spec(block_rows, block_cols, slot):
        return pl.BlockSpec((None, None, block_rows, block_cols),
                            lambda b, j, pt: (layer, pt[b, j * pages + slot], 0, 0))

    per_b = lambda b, j, pt: (b, 0, 0)
    grid_spec = pltpu.PrefetchScalarGridSpec(
        num_scalar_prefetch=1,
        grid=(batch, n_pages // pages),
        in_specs=[
            pl.BlockSpec(lam_params.shape, lambda b, j, pt: (0, 0)),
            pl.BlockSpec((heads, 1, vd), lambda b, j, pt: (0, 0, 0)),
            pl.BlockSpec((1, rows, d), per_b),
            pl.BlockSpec((1, new_keys, d), per_b),
            pl.BlockSpec((1, new_keys * heads, vd), per_b),
        ] + [page_spec(d, page, s) for s in range(pages)]
        + [page_spec(page * heads, vd, s) for s in range(pages)],
        out_specs=pl.BlockSpec((1, heads, tn, vd), lambda b, j, pt: (b, 0, 0, 0)),
        scratch_shapes=[pltpu.VMEM((rows, LANES), F32), pltpu.VMEM((rows, LANES), F32),
                        pltpu.VMEM((rows, vd), F32)],
    )
    out = pl.pallas_call(
        functools.partial(_paged_kernel, pages=pages, heads=heads, tn=tn, lam_init=lam_init),
        grid_spec=grid_spec,
        out_shape=jax.ShapeDtypeStruct((batch, heads, tn, vd), F32),
        compiler_params=_params("parallel", "arbitrary"),
        name="paged_attn",
    )(page_table, lam_params, head_gain.reshape(heads, 1, vd), qx, kn, vn, *([ck] * pages), *([cv] * pages))
    return out.transpose(0, 2, 1, 3).reshape(n, heads * vd)


def _post_kernel(x_ref, d_ref, wo_ref, g_ref, wg_ref, wu_ref, wd_ref, fg_ref, o_ref, *, chunks, final):
    x1 = x_ref[...] + _dot(d_ref[...].astype(BF16), wo_ref[...])
    h = _rms(x1, g_ref[...]).astype(BF16)
    acc = x1
    for c0, c1 in chunks:
        g = _dot(h, wg_ref[:, c0:c1])
        u = _dot(h, wu_ref[:, c0:c1])
        a = (g / (1.0 + jnp.exp(-g)) * u).astype(BF16)
        acc = acc + _dot(a, wd_ref[c0:c1, :])
    o_ref[...] = _rms(acc, fg_ref[...]) if final else acc


def _post(x2d, d2d, wo, ffn_gain, wg, wu, wd, final_gain, final):
    n, dm = x2d.shape
    f = wg.shape[1]
    tm = _row_tile(n)
    step = 1024
    chunks = tuple((c, min(c + step, f)) for c in range(0, f, step))
    row = lambda i: (i, 0)
    return pl.pallas_call(
        functools.partial(_post_kernel, chunks=chunks, final=final),
        grid=(n // tm,),
        in_specs=[pl.BlockSpec((tm, dm), row), pl.BlockSpec((tm, dm), row), _resident(wo.shape),
                  _resident((1, dm)), _resident(wg.shape), _resident(wu.shape), _resident(wd.shape),
                  _resident((1, dm))],
        out_specs=pl.BlockSpec((tm, dm), row),
        out_shape=jax.ShapeDtypeStruct((n, dm), F32),
        compiler_params=_params("parallel"),
        name="post_final" if final else "post",
    )(x2d, d2d, wo, ffn_gain.reshape(1, dm), wg, wu, wd, final_gain.reshape(1, dm))


def _log_gates(g, is_forget):
    ig = GATE_SOFTCAP * jnp.tanh(g / GATE_SOFTCAP)
    lf = jnp.minimum(g, 0.0) - jnp.log1p(jnp.exp(-jnp.abs(g)))
    return jnp.where(is_forget, lf, ig)


def _mlstm_proj_kernel(x_ref, g_ref, w_ref, wkt_ref, wgt_ref, bc_ref, br_ref,
                       q_ref, kt_ref, v_ref, og_ref, gc_ref, gr_ref, *, hk, hv, q_scale):
    h = _rms(x_ref[...], g_ref[...]).astype(BF16)
    q_ref[...] = _dot(h, w_ref[:, :hk]) * q_scale
    kt_ref[...] = _nt_dot(wkt_ref[...], h)
    v_ref[...] = _dot(h, w_ref[:, 2 * hk:2 * hk + hv])
    og_ref[...] = _dot(h, w_ref[:, 2 * hk + hv:])
    wgt = wgt_ref[...]
    ng = wgt.shape[0]
    gcol = lax.dot_general(h, wgt, (((1,), (1,)), ((), ())), preferred_element_type=F32) + bc_ref[...]
    grow = _nt_dot(wgt, h) + br_ref[...]
    gc_ref[...] = _log_gates(gcol, lax.broadcasted_iota(jnp.int32, gcol.shape, 1) >= ng // 2)
    gr_ref[...] = _log_gates(grow, lax.broadcasted_iota(jnp.int32, grow.shape, 0) >= ng // 2)


def _mlstm_proj(x2d, gain, w_main, w_k_t, w_gates_t, b_gates, hk, hv, q_scale):
    n, d = x2d.shape
    tm = _row_tile(n)
    ng = w_gates_t.shape[0]
    row = lambda i: (i, 0)
    col = lambda i: (0, i)
    return pl.pallas_call(
        functools.partial(_mlstm_proj_kernel, hk=hk, hv=hv, q_scale=q_scale),
        grid=(n // tm,),
        in_specs=[pl.BlockSpec((tm, d), row), _resident((1, d)), _resident(w_main.shape),
                  _resident(w_k_t.shape), _resident(w_gates_t.shape), _resident((1, ng)), _resident((ng, 1))],
        out_specs=[pl.BlockSpec((tm, hk), row), pl.BlockSpec((hk, tm), col), pl.BlockSpec((tm, hv), row),
                   pl.BlockSpec((tm, d), row), pl.BlockSpec((tm, ng), row), pl.BlockSpec((ng, tm), col)],
        out_shape=[jax.ShapeDtypeStruct((n, hk), F32), jax.ShapeDtypeStruct((hk, n), F32),
                   jax.ShapeDtypeStruct((n, hv), F32), jax.ShapeDtypeStruct((n, d), F32),
                   jax.ShapeDtypeStruct((n, ng), F32), jax.ShapeDtypeStruct((ng, n), F32)],
        compiler_params=_params("parallel"),
        name="mlstm_proj",
    )(x2d, gain.reshape(1, d), w_main, w_k_t, w_gates_t, b_gates.reshape(1, ng), b_gates.reshape(ng, 1))


def _split3(x):
    hi = x.astype(BF16)
    r1 = x - hi.astype(F32)
    mid = r1.astype(BF16)
    lo = (r1 - mid.astype(F32)).astype(BF16)
    return hi, mid, lo


def _mlstm_kernel(q_ref, kt_ref, v_ref, og_ref, gc_ref, gr_ref, gain_ref, s0_ref, m0_ref,
                  y_ref, s_out_ref, m_out_ref, s_sc, m_sc, *, heads):
    c = pl.program_id(1)
    L = q_ref.shape[0]
    dk = q_ref.shape[1] // heads
    dk2 = 2 * dk
    dv = v_ref.shape[1] // heads

    @pl.when(c == 0)
    def _():
        for pr in range(heads // 2):
            s_sc[pr] = jnp.concatenate([s0_ref[0, 2 * pr], s0_ref[0, 2 * pr + 1]], axis=0)
        m_sc[...] = m0_ref[0]

    tt = lax.broadcasted_iota(jnp.int32, (L, L), 0)
    ss = lax.broadcasted_iota(jnp.int32, (L, L), 1)
    causal = ss <= tt
    tril = jnp.where(causal, 1.0, 0.0).astype(BF16)
    triu = jnp.where(tt <= ss, 1.0, 0.0).astype(BF16)

    gc = gc_ref[...]
    gr = gr_ref[0]
    bcol = sum(_dot(tril, part) for part in _split3(gc))
    brow = sum(_dot(part, triu) for part in _split3(gr))
    lane = lax.broadcasted_iota(jnp.int32, (L, dk2), 1)
    sub = lax.broadcasted_iota(jnp.int32, (dk2, L), 0)
    one_col = jnp.where(lax.broadcasted_iota(jnp.int32, (L, dv), 1) == 0, 1.0, 0.0)
    last = c == pl.num_programs(1) - 1

    H = range(heads)
    P = range(heads // 2)
    qs = [q_ref[:, pr * dk2:(pr + 1) * dk2].astype(BF16) for pr in P]
    kts = [kt_ref[0, pr * dk2:(pr + 1) * dk2, :].astype(BF16) for pr in P]
    i_col = [gc[:, h:h + 1] for h in H]
    b_col = [bcol[:, heads + h:heads + h + 1] for h in H]
    i_row = [gr[h:h + 1, :] for h in H]
    b_row = [brow[heads + h:heads + h + 1, :] for h in H]
    b_last = [b[:, L - 1:L] for b in b_row]

    u = [jnp.where(causal, i_row[h] - b_row[h], -jnp.inf) for h in H]
    cm = [jnp.max(x, axis=-1, keepdims=True) for x in u]
    e = [jnp.exp(u[h] - cm[h]) for h in H]
    a = [b_col[h] + cm[h] for h in H]
    qz = [jnp.where((lane >= (h % 2) * dk) & (lane < (h % 2 + 1) * dk), qs[h // 2], jnp.zeros_like(qs[0]))
          for h in H]
    qk = [_dot(qz[h], kts[h // 2]) for h in H]
    g = [(qk[h] * e[h]).astype(BF16) for h in H]
    vx = [jnp.concatenate([v_ref[:, h * dv:(h + 1) * dv], one_col], axis=1) for h in H]
    intra = [_dot(g[h], vx[h].astype(BF16)) for h in H]

    m_loc = [jnp.max(b_last[h] - b_row[h] + i_row[h], axis=-1, keepdims=True) for h in H]
    ws_col = [jnp.exp(b_last[h] - b_col[h] + i_col[h] - m_loc[h]) for h in H]
    ktz = [jnp.where((sub >= (h % 2) * dk) & (sub < (h % 2 + 1) * dk), kts[h // 2], jnp.zeros_like(kts[0]))
           for h in H]
    upd = [_dot(ktz[h], (ws_col[h] * vx[h]).astype(BF16)) for h in H]

    state = [s_sc[pr] for pr in P]
    state_bf = [x.astype(BF16) for x in state]
    inter_c = [_dot(qz[h], state_bf[h // 2]) for h in H]
    m_prev = [m_sc[h][:, :1] for h in H]
    inter = [b_col[h] + m_prev[h] for h in H]
    mt = [jnp.maximum(inter[h], a[h]) for h in H]
    nd = [jnp.exp(inter[h] - mt[h]) * inter_c[h] + jnp.exp(a[h] - mt[h]) * intra[h] for h in H]
    inv = [1.0 / jnp.maximum(jnp.abs(nd[h][:, dv:dv + 1]), jnp.exp(-mt[h])) for h in H]
    msq = [jnp.mean(nd[h][:, :dv] * nd[h][:, :dv], axis=-1, keepdims=True) for h in H]
    scale = [inv[h] * lax.rsqrt(msq[h] * inv[h] * inv[h] + RMS_EPS) for h in H]
    sig = [1.0 + jnp.exp(-og_ref[:, h * dv:(h + 1) * dv]) for h in H]
    for h in H:
        y_ref[:, h * dv:(h + 1) * dv] = (nd[h][:, :dv] * scale[h] * gain_ref[h] / sig[h]).astype(y_ref.dtype)

    m_new = [jnp.maximum(b_last[h] + m_prev[h], m_loc[h]) for h in H]
    gain_new = [jnp.exp(m_loc[h] - m_new[h]) for h in H]
    decay = [jnp.exp(b_last[h] + m_prev[h] - m_new[h]) for h in H]
    row_id = lax.broadcasted_iota(jnp.int32, state[0].shape, 0)
    for pr in P:
        h0, h1 = 2 * pr, 2 * pr + 1
        s_sc[pr] = (gain_new[h0] * upd[h0] + gain_new[h1] * upd[h1]
                    + jnp.where(row_id < dk, decay[h0], decay[h1]) * state[pr])
    for h in H:
        m_sc[h] = jnp.broadcast_to(m_new[h], m_sc.shape[1:])

    @pl.when(last)
    def _():
        for pr in range(heads // 2):
            s_out_ref[0, 2 * pr] = s_sc[pr, :dk]
            s_out_ref[0, 2 * pr + 1] = s_sc[pr, dk:]
        m_out_ref[0] = m_sc[...]


def _mlstm(q, kt, v, og, gc, gr, head_gain, s0, m0, batch, nc):
    n, hk = q.shape
    heads, dv = head_gain.shape
    dk = hk // heads
    L = n // (batch * nc)
    ng = gc.shape[1]
    blk = lambda b, c: (b * nc + c, 0)
    st = lambda b, c: (b, 0, 0, 0)
    if kt.shape[0] == 1:
        lanes_blk = lambda b, c: (0, 0, b * nc + c)
    else:
        lanes_blk = lambda b, c: (b * nc + c, 0, 0)
    y, s_out, m_out = pl.pallas_call(
        functools.partial(_mlstm_kernel, heads=heads),
        grid=(batch, nc),
        in_specs=[
            pl.BlockSpec((L, hk), blk), pl.BlockSpec((1, hk, L), lanes_blk),
            pl.BlockSpec((L, heads * dv), blk), pl.BlockSpec((L, heads * dv), blk),
            pl.BlockSpec((L, ng), blk),
            pl.BlockSpec((1, ng, L), lanes_blk),
            pl.BlockSpec((heads, 1, dv), lambda b, c: (0, 0, 0)),
            pl.BlockSpec((1, heads, dk, 2 * dv), st),
            pl.BlockSpec((1, heads, 1, LANES), st),
        ],
        out_specs=[pl.BlockSpec((L, heads * dv), blk), pl.BlockSpec((1, heads, dk, 2 * dv), st),
                   pl.BlockSpec((1, heads, 1, LANES), st)],
        out_shape=[jax.ShapeDtypeStruct((n, heads * dv), BF16),
                   jax.ShapeDtypeStruct((batch, heads, dk, 2 * dv), F32),
                   jax.ShapeDtypeStruct((batch, heads, 1, LANES), F32)],
        scratch_shapes=[pltpu.VMEM((heads // 2, 2 * dk, 2 * dv), F32), pltpu.VMEM((heads, 1, LANES), F32)],
        compiler_params=_params("parallel", "arbitrary"),
        name="mlstm",
    )(q, kt, v, og, gc, gr, head_gain.reshape(heads, 1, dv), s0, m0)
    return y, s_out[..., :dv], s_out[..., dv], m_out[:, :, 0, 0]


def _pack_state(C, nvec, m):
    dv = C.shape[-1]
    s0 = jnp.concatenate([C, nvec[..., None], jnp.zeros(C.shape[:-1] + (dv - 1,), F32)], axis=-1)
    return s0, jnp.broadcast_to(m[..., None, None], m.shape + (1, LANES))


def _pad_chunks(a, batch, t, L, value=0.0):
    w = a.shape[-1]
    return jnp.pad(a.reshape(batch, t, w), ((0, 0), (0, L - t), (0, 0)), constant_values=value).reshape(batch * L, w)


def _mlstm_layer(x2d, batch, t, norm_gain, w_main, w_k_t, w_gates_t, b_gates, head_gain, C0, n0, m0):
    heads, dv = head_gain.shape
    ng = 2 * heads
    hk = w_k_t.shape[0]
    dk = hk // heads
    q, kt, v, og, gc, gr = _mlstm_proj(x2d, norm_gain, w_main, w_k_t, w_gates_t, b_gates, hk, heads * dv,
                                       dk ** -0.5)
    s0, m0p = _pack_state(C0, n0, m0)
    if t % ML_CHUNK == 0:
        y, C, nvec, m = _mlstm(q, kt[None], v, og, gc, gr[None], head_gain, s0, m0p, batch, t // ML_CHUNK)
        return y, C, nvec, m
    L = -(-t // SHORT_CHUNK) * SHORT_CHUNK
    pad = lambda a: _pad_chunks(a, batch, t, L)
    gate_pad = jnp.concatenate([jnp.full((heads,), NEG_BIG, F32), jnp.zeros((heads,), F32)])
    gcp = jnp.where((jnp.arange(batch * L) % L < t)[:, None], pad(gc), gate_pad[None, :])
    grp = gcp.reshape(batch, L, ng).transpose(0, 2, 1)
    ktp = jnp.pad(kt.reshape(hk, batch, t).transpose(1, 0, 2), ((0, 0), (0, 0), (0, L - t)))
    y, C, nvec, m = _mlstm(pad(q), ktp, pad(v), pad(og), gcp, grp, head_gain, s0, m0p, batch, 1)
    y = y.reshape(batch, L, -1)[:, :t].reshape(batch * t, -1)
    return y, C, nvec, m


def kernel(x_prompt, x_sample, cache_k, cache_v, page_table, state_C, state_n, state_m, attn_norm, w_attn_qkv, attn_lambda, attn_head_gain, w_attn_out, mlstm_norm, w_mlstm_in, b_mlstm_gates, mlstm_head_gain, w_mlstm_out, ffn_norm, w_ffn_gate, w_ffn_up, w_ffn_down, final_norm):
    bp, tp, dm = x_prompt.shape
    bs, ts, _ = x_sample.shape
    depth = ffn_norm.shape[0]
    n_mixers = 2
    da_heads, da_vd = attn_head_gain.shape[1:]
    ml_heads, ml_dv = mlstm_head_gain.shape[1:]
    nqk = (w_attn_qkv.shape[2] - da_heads * da_vd) // 2
    da_hd = nqk // (2 * da_heads)
    past_len = page_table.shape[1] * cache_k.shape[2]

    tab_p, tab_pt = _rope_tables(jnp.arange(tp, dtype=F32))
    tab_s = tuple(jnp.tile(a, (bs, 1)) for a in _rope_tables(past_len + jnp.arange(ts, dtype=F32))[0])

    xp = x_prompt.reshape(bp * tp, dm)
    xs = x_sample.reshape(bs * ts, dm)
    k_p, v_p, k_s, v_s = [], [], [], []
    c_p, n_p, m_p, c_s, n_s, m_s = [], [], [], [], [], []
    for i in range(depth):
        final = i == depth - 1
        if i % n_mixers == 0:
            a = i // n_mixers
            lam_init = 0.8 - 0.6 * math.exp(-0.3 * i)
            q_scale = da_hd ** -0.5 * LOG2E
            w_qkv = w_attn_qkv[a].astype(BF16)
            w_out = w_attn_out[a].astype(BF16)
            wkt = w_attn_qkv[a][:, nqk:2 * nqk].T.astype(BF16)
            q, kt, v = _attn_proj_kt(xp, attn_norm[a], w_qkv, wkt, tab_p, tab_pt, bp, nqk, da_hd,
                                     da_heads * da_vd, q_scale)
            dp = _flash_attention(q, kt, v, attn_lambda[a], attn_head_gain[a], bp, tp, lam_init)
            k_p.append(kt.reshape(bp, 2 * da_heads, da_hd, tp).transpose(0, 3, 1, 2))
            v_p.append(v.reshape(bp, tp, da_heads, da_vd))
            q, k, v = _attn_proj(xs, attn_norm[a], w_qkv, tab_s, nqk, da_heads * da_vd, q_scale)
            ds = _paged_attention(q, k, v, cache_k, cache_v, a, page_table, attn_lambda[a],
                                  attn_head_gain[a], ts, lam_init)
            k_s.append(k.reshape(bs, ts, 2 * da_heads, da_hd))
            v_s.append(v.reshape(bs, ts, da_heads, da_vd))
        else:
            j = i // n_mixers
            hv = ml_heads * ml_dv
            w_in = w_mlstm_in[j]
            w_main = w_in[:, :w_in.shape[1] - 2 * ml_heads].astype(BF16)
            w_gates_t = w_in[:, w_in.shape[1] - 2 * ml_heads:].T.astype(BF16)
            w_out = w_mlstm_out[j].astype(BF16)
            hk = (w_main.shape[1] - hv - dm) // 2
            dk = hk // ml_heads
            w_k_t = w_in[:, hk:2 * hk].T.astype(BF16)
            zero = (jnp.zeros((bp, ml_heads, dk, ml_dv), F32), jnp.zeros((bp, ml_heads, dk), F32),
                    jnp.zeros((bp, ml_heads), F32))
            dp, C, nv, m = _mlstm_layer(xp, bp, tp, mlstm_norm[j], w_main, w_k_t, w_gates_t, b_mlstm_gates[j],
                                        mlstm_head_gain[j], *zero)
            c_p.append(C); n_p.append(nv); m_p.append(m)
            ds, C, nv, m = _mlstm_layer(xs, bs, ts, mlstm_norm[j], w_main, w_k_t, w_gates_t, b_mlstm_gates[j],
                                        mlstm_head_gain[j], state_C[j], state_n[j], state_m[j])
            c_s.append(C); n_s.append(nv); m_s.append(m)
        wg, wu, wd = (w[i].astype(BF16) for w in (w_ffn_gate, w_ffn_up, w_ffn_down))
        xp = _post(xp, dp, w_out, ffn_norm[i], wg, wu, wd, final_norm, final)
        xs = _post(xs, ds, w_out, ffn_norm[i], wg, wu, wd, final_norm, final)
    return (xp.reshape(bp, tp, dm), xs.reshape(bs, ts, dm),
            jnp.stack(k_p), jnp.stack(v_p), jnp.stack(c_p), jnp.stack(n_p), jnp.stack(m_p),
            jnp.stack(k_s), jnp.stack(v_s), jnp.stack(c_s), jnp.stack(n_s), jnp.stack(m_s))
```

```python
import functools
import math

import jax
import jax.numpy as jnp
from jax import lax
from jax.experimental import pallas as pl
from jax.experimental.pallas import tpu as pltpu

RMS_EPS = 1e-6
ROPE_THETA = 10000.0
GATE_SOFTCAP = 15.0
ML_CHUNK = 128
SHORT_CHUNK = 16
LANES = 128
NEG_BIG = -1e30
LOG2E = math.log2(math.e)
VMEM_LIMIT_BYTES = 56 * 1024 * 1024

F32 = jnp.float32
BF16 = jnp.bfloat16


def _params(*sem):
    return pltpu.CompilerParams(dimension_semantics=sem, vmem_limit_bytes=VMEM_LIMIT_BYTES)


def _resident(shape):
    nd = len(shape)
    return pl.BlockSpec(shape, lambda *_: (0,) * nd, pipeline_mode=pl.Buffered(1))


def _row_tile(n):
    return 512 if n % 512 == 0 else n


def _rms(x, gain):
    return x * lax.rsqrt(jnp.mean(x * x, axis=-1, keepdims=True) + RMS_EPS) * gain


def _nt_dot(a, b):
    return lax.dot_general(a, b, (((1,), (1,)), ((), ())), preferred_element_type=F32)


def _dot(a, b):
    return jnp.dot(a, b, preferred_element_type=F32)


def _rope_tables(pos):
    half = 32
    inv = ROPE_THETA ** (-jnp.arange(half, dtype=F32) * 2.0 / (2 * half))
    ang = pos[:, None] * inv[None, :]
    cos, sin = jnp.cos(ang), jnp.sin(ang)
    zero = jnp.zeros_like(sin)
    cos_t = jnp.tile(cos, (1, 4))
    sin_lo = jnp.tile(jnp.concatenate([-sin, zero], axis=1), (1, 2))
    sin_hi = jnp.tile(jnp.concatenate([zero, sin], axis=1), (1, 2))
    return (cos_t, sin_lo, sin_hi), (cos.T, sin.T)


def _rope_lanes(y, o_ref, cos, slo, shi, scale):
    for c in range(y.shape[1] // LANES):
        yc = y[:, c * LANES:(c + 1) * LANES]
        r = yc * cos + pltpu.roll(yc, LANES - 32, 1) * slo + pltpu.roll(yc, 32, 1) * shi
        o_ref[:, c * LANES:(c + 1) * LANES] = r if scale == 1.0 else r * scale


def _attn_proj_kernel(x_ref, g_ref, w_ref, cos_ref, slo_ref, shi_ref, q_ref, k_ref, v_ref, *, nqk, q_scale):
    h = _rms(x_ref[...], g_ref[...]).astype(BF16)
    cos, slo, shi = cos_ref[...], slo_ref[...], shi_ref[...]
    _rope_lanes(_dot(h, w_ref[:, :nqk]), q_ref, cos, slo, shi, q_scale)
    _rope_lanes(_dot(h, w_ref[:, nqk:2 * nqk]), k_ref, cos, slo, shi, 1.0)
    v_ref[...] = _dot(h, w_ref[:, 2 * nqk:])


def _attn_proj_kt_kernel(x_ref, g_ref, w_ref, wkt_ref, cos_ref, slo_ref, shi_ref, cst_ref, snt_ref,
                         q_ref, kt_ref, v_ref, *, nqk, hd, q_scale):
    h = _rms(x_ref[...], g_ref[...]).astype(BF16)
    _rope_lanes(_dot(h, w_ref[:, :nqk]), q_ref, cos_ref[...], slo_ref[...], shi_ref[...], q_scale)
    kt = _nt_dot(wkt_ref[...], h)
    cs, sn = cst_ref[...], snt_ref[...]
    half = hd // 2
    for mp in range(nqk // hd):
        x1 = kt[mp * hd:mp * hd + half]
        x2 = kt[mp * hd + half:(mp + 1) * hd]
        kt_ref[mp * hd:mp * hd + half, :] = x1 * cs - x2 * sn
        kt_ref[mp * hd + half:(mp + 1) * hd, :] = x2 * cs + x1 * sn
    v_ref[...] = _dot(h, w_ref[:, 2 * nqk:])


def _attn_proj(x2d, gain, w_bf16, tables, nqk, nv, q_scale):
    n, d = x2d.shape
    tm = _row_tile(n)
    period = tables[0].shape[0] // tm
    row = lambda i: (i, 0)
    tab = pl.BlockSpec((tm, LANES), lambda i: (i % period, 0))
    return pl.pallas_call(
        functools.partial(_attn_proj_kernel, nqk=nqk, q_scale=q_scale),
        grid=(n // tm,),
        in_specs=[pl.BlockSpec((tm, d), row), _resident((1, d)), _resident(w_bf16.shape), tab, tab, tab],
        out_specs=[pl.BlockSpec((tm, nqk), row), pl.BlockSpec((tm, nqk), row), pl.BlockSpec((tm, nv), row)],
        out_shape=[jax.ShapeDtypeStruct((n, nqk), F32), jax.ShapeDtypeStruct((n, nqk), F32),
                   jax.ShapeDtypeStruct((n, nv), F32)],
        compiler_params=_params("parallel"),
        name="attn_proj",
    )(x2d, gain.reshape(1, d), w_bf16, *tables)


def _attn_proj_kt(x2d, gain, w_bf16, wkt_bf16, tables, tables_t, batch, nqk, hd, nv, q_scale):
    n, d = x2d.shape
    seq = n // batch
    tm = _row_tile(seq)
    period = seq // tm
    row = lambda i: (i, 0)
    tab = pl.BlockSpec((tm, LANES), lambda i: (i % period, 0))
    tab_t = pl.BlockSpec((hd // 2, tm), lambda i: (0, i % period))
    return pl.pallas_call(
        functools.partial(_attn_proj_kt_kernel, nqk=nqk, hd=hd, q_scale=q_scale),
        grid=(n // tm,),
        in_specs=[pl.BlockSpec((tm, d), row), _resident((1, d)), _resident(w_bf16.shape),
                  _resident(wkt_bf16.shape), tab, tab, tab, tab_t, tab_t],
        out_specs=[pl.BlockSpec((tm, nqk), row),
                   pl.BlockSpec((None, nqk, tm), lambda i: (i // period, 0, i % period)),
                   pl.BlockSpec((tm, nv), row)],
        out_shape=[jax.ShapeDtypeStruct((n, nqk), F32), jax.ShapeDtypeStruct((batch, nqk, seq), F32),
                   jax.ShapeDtypeStruct((n, nv), F32)],
        compiler_params=_params("parallel"),
        name="attn_proj_kt",
    )(x2d, gain.reshape(1, d), w_bf16, wkt_bf16, *tables, *tables_t)


def _lambda(lam_ref, lam_init):
    lp = lam_ref[...]
    a = jnp.sum(lp[0:1, :] * lp[1:2, :], axis=-1, keepdims=True)
    b = jnp.sum(lp[2:3, :] * lp[3:4, :], axis=-1, keepdims=True)
    return jnp.exp(a) - jnp.exp(b) + lam_init


def _flash_kernel(*refs, **static):
    _flash_body(pl.program_id(2), *refs, **static)


def _flash_body(i, lam_ref, gain_ref, q_ref, kt_ref, v_ref, o_ref, k_sc, v_sc, qq_sc, m_sc, l_sc, acc_sc,
                *, tq, hd, lam_init):
    @pl.when(i == 0)
    def _():
        for c in range(k_sc.shape[0]):
            k_sc[c] = kt_ref[:, c * tq:(c + 1) * tq].astype(BF16)
            v_sc[c] = v_ref[c * tq:(c + 1) * tq, :].astype(BF16)

    q = q_ref[...].astype(BF16)
    lane = lax.broadcasted_iota(jnp.int32, q.shape, 1)
    zero = jnp.zeros_like(q)
    qq_sc[:tq] = jnp.where(lane < hd, q, zero)
    qq_sc[tq:] = jnp.where(lane >= hd, q, zero)
    m_sc[...] = jnp.full(m_sc.shape, -jnp.inf, F32)
    l_sc[...] = jnp.zeros(l_sc.shape, F32)
    acc_sc[...] = jnp.zeros(acc_sc.shape, F32)

    def update(j, diagonal):
        s = _dot(qq_sc[...], k_sc[j])
        if diagonal:
            r = lax.broadcasted_iota(jnp.int32, (2 * tq, tq), 0)
            c = lax.broadcasted_iota(jnp.int32, (2 * tq, tq), 1)
            s = jnp.where(c <= jnp.where(r >= tq, r - tq, r), s, -jnp.inf)
        m_prev = m_sc[...]
        m_new = jnp.maximum(m_prev, jnp.max(s, axis=-1, keepdims=True))
        alpha = jnp.exp2(m_prev - m_new)
        p = jnp.exp2(s - jnp.concatenate([m_new] * (tq // LANES), axis=1))
        acc_sc[...] = acc_sc[...] * alpha + _dot(p.astype(BF16), v_sc[j])
        l_sc[...] = l_sc[...] * alpha + jnp.sum(p, axis=-1, keepdims=True)
        m_sc[...] = m_new

    def pair(jj, carry):
        update(2 * jj, False)
        update(2 * jj + 1, False)
        return carry

    lax.fori_loop(0, i // 2, pair, 0)

    @pl.when(i % 2 == 1)
    def _():
        update(i - 1, False)

    update(i, True)

    o = acc_sc[...] / l_sc[...]
    d = o[:tq] - _lambda(lam_ref, lam_init) * o[tq:]
    d = d * lax.rsqrt(jnp.mean(d * d, axis=-1, keepdims=True) + RMS_EPS)
    o_ref[...] = (d * gain_ref[0] * (1.0 - lam_init)).astype(o_ref.dtype)


def _flash_attention(q, kt, v, lam_params, head_gain, batch, seq, lam_init):
    n, nqk = q.shape
    heads, vd = head_gain.shape
    hd = nqk // (2 * heads)
    tq = min(512, seq)
    nq = seq // tq
    assert vd == LANES and 2 * hd == LANES and tq % LANES == 0
    return pl.pallas_call(
        functools.partial(_flash_kernel, tq=tq, hd=hd, lam_init=lam_init),
        grid=(batch, heads, nq),
        in_specs=[
            _resident(lam_params.shape),
            pl.BlockSpec((1, 1, vd), lambda b, h, i: (h, 0, 0)),
            pl.BlockSpec((tq, 2 * hd), lambda b, h, i: (b * nq + i, h)),
            pl.BlockSpec((None, 2 * hd, seq), lambda b, h, i: (b, h, 0)),
            pl.BlockSpec((seq, vd), lambda b, h, i: (b, h)),
        ],
        out_specs=pl.BlockSpec((tq, vd), lambda b, h, i: (b * nq + i, h)),
        out_shape=jax.ShapeDtypeStruct((n, heads * vd), BF16),
        scratch_shapes=[pltpu.VMEM((nq, 2 * hd, tq), BF16), pltpu.VMEM((nq, tq, vd), BF16),
                        pltpu.VMEM((2 * tq, 2 * hd), BF16), pltpu.VMEM((2 * tq, LANES), F32),
                        pltpu.VMEM((2 * tq, LANES), F32), pltpu.VMEM((2 * tq, vd), F32)],
        compiler_params=_params("parallel", "parallel", "arbitrary"),
        name="flash_attn",
    )(lam_params, head_gain.reshape(heads, 1, vd), q, kt, v)


def _paged_kernel(pt_ref, lam_ref, gain_ref, qx_ref, kn_ref, vn_ref, *rest, pages, heads, tn, lam_init):
    k_refs, v_refs = rest[:pages], rest[pages:2 * pages]
    o_ref, m_sc, l_sc, acc_sc = rest[2 * pages:]
    j = pl.program_id(1)
    _paged_body(j, j == pl.num_programs(1) - 1, lam_ref, gain_ref, qx_ref, kn_ref, vn_ref, k_refs, v_refs,
                o_ref, m_sc, l_sc, acc_sc, heads=heads, tn=tn, lam_init=lam_init)


def _paged_body(j, is_last, lam_ref, gain_ref, qx_ref, kn_ref, vn_ref, k_refs, v_refs,
                o_ref, m_sc, l_sc, acc_sc, *, heads, tn, lam_init):
    pages = len(k_refs)
    qx = qx_ref[0]
    rows = qx.shape[0]
    rph = rows // heads
    keys = kn_ref.shape[1]

    @pl.when(j == 0)
    def _():
        m_sc[...] = jnp.full(m_sc.shape, -jnp.inf, F32)
        l_sc[...] = jnp.zeros(l_sc.shape, F32)
        acc_sc[...] = jnp.zeros(acc_sc.shape, F32)

    def update(ss, v_list):
        m = m_sc[:, :1]
        m_new = m
        for s in ss:
            m_new = jnp.maximum(m_new, jnp.max(s, axis=-1, keepdims=True))
        alpha = jnp.exp2(m - m_new)
        l = l_sc[:, :1] * alpha
        acc = acc_sc[...] * alpha
        for s, v_ref in zip(ss, v_list):
            p = jnp.exp2(s - m_new)
            l = l + jnp.sum(p, axis=-1, keepdims=True)
            pb = p.astype(BF16)
            acc = acc + jnp.concatenate(
                [_dot(pb[h * rph:(h + 1) * rph], v_ref[pl.ds(h, s.shape[1], stride=heads), :].astype(BF16))
                 for h in range(heads)], axis=0)
        m_sc[...] = jnp.broadcast_to(m_new, m_sc.shape)
        l_sc[...] = jnp.broadcast_to(l, l_sc.shape)
        acc_sc[...] = acc

    group = 8
    for g0 in range(0, pages, group):
        update([_dot(qx, r[...].astype(BF16)) for r in k_refs[g0:g0 + group]], v_refs[g0:g0 + group])

    @pl.when(is_last)
    def _():
        t = lax.broadcasted_iota(jnp.int32, (rows, keys), 0) % tn
        c = lax.broadcasted_iota(jnp.int32, (rows, keys), 1)
        s_new = _nt_dot(qx, kn_ref[0].astype(BF16))
        update([jnp.where(c <= t, s_new, -jnp.inf)], [vn_ref.at[0]])

        o = (acc_sc[...] / l_sc[:, :1]).reshape(heads, rph, acc_sc.shape[1])
        d = o[:, :tn, :] - _lambda(lam_ref, lam_init) * o[:, tn:, :]
        d = d * lax.rsqrt(jnp.mean(d * d, axis=-1, keepdims=True) + RMS_EPS)
        o_ref[0] = d * gain_ref[...] * (1.0 - lam_init)


def _paged_operands(q, k_new, v_new, cache_k, cache_v, heads, vd, tn):
    n, d = q.shape
    batch = n // tn
    hd = d // (2 * heads)
    page = cache_k.shape[2]
    rows = tn * 2 * heads

    r = jnp.arange(rows)
    owner = r // tn
    lane_map = jnp.arange(d) // hd
    qx = jnp.where(owner[None, :, None] == lane_map[None, None, :],
                   jnp.tile(q.reshape(batch, tn, d), (1, 2 * heads, 1)), 0.0).astype(BF16)

    new_keys = -(-tn // SHORT_CHUNK) * SHORT_CHUNK
    kn = jnp.pad(k_new.reshape(batch, tn, d), ((0, 0), (0, new_keys - tn), (0, 0)))
    vn = jnp.pad(v_new.reshape(batch, tn, heads, vd), ((0, 0), (0, new_keys - tn), (0, 0), (0, 0)))
    vn = vn.reshape(batch, new_keys * heads, vd)

    nl, n_pool = cache_k.shape[:2]
    ck = cache_k.transpose(0, 1, 3, 4, 2).reshape(nl, n_pool, d, page)
    cv = cache_v.reshape(nl, n_pool, page * heads, vd)
    return qx, kn, vn, ck, cv


def _paged_attention(q, k_new, v_new, cache_k, cache_v, layer, page_table, lam_params, head_gain, tn, lam_init):
    n, d = q.shape
    batch = n // tn
    heads, vd = head_gain.shape
    n_pages = page_table.shape[1]
    page = cache_k.shape[2]
    pages = next(p for p in (16, 8, 4, 2, 1) if n_pages % p == 0)
    rows = tn * 2 * heads
    qx, kn, vn, ck, cv = _paged_operands(q, k_new, v_new, cache_k, cache_v, heads, vd, tn)
    new_keys = kn.shape[1]

    def page_spec(block_rows, block_cols, slot):
        return pl.BlockSpec((None, None, block_rows, block_cols),
                            lambda b, j, pt: (layer, pt[b, j * pages + slot], 0, 0))

    per_b = lambda b, j, pt: (b, 0, 0)
    grid_spec = pltpu.PrefetchScalarGridSpec(
        num_scalar_prefetch=1,
        grid=(batch, n_pages // pages),
        in_specs=[
            pl.BlockSpec(lam_params.shape, lambda b, j, pt: (0, 0)),
            pl.BlockSpec((heads, 1, vd), lambda b, j, pt: (0, 0, 0)),
            pl.BlockSpec((1, rows, d), per_b),
            pl.BlockSpec((1, new_keys, d), per_b),
            pl.BlockSpec((1, new_keys * heads, vd), per_b),
        ] + [page_spec(d, page, s) for s in range(pages)]
        + [page_spec(page * heads, vd, s) for s in range(pages)],
        out_specs=pl.BlockSpec((1, heads, tn, vd), lambda b, j, pt: (b, 0, 0, 0)),
        scratch_shapes=[pltpu.VMEM((rows, LANES), F32), pltpu.VMEM((rows, LANES), F32),
                        pltpu.VMEM((rows, vd), F32)],
    )
    out = pl.pallas_call(
        functools.partial(_paged_kernel, pages=pages, heads=heads, tn=tn, lam_init=lam_init),
        grid_spec=grid_spec,
        out_shape=jax.ShapeDtypeStruct((batch, heads, tn, vd), F32),
        compiler_params=_params("parallel", "arbitrary"),
        name="paged_attn",
    )(page_table, lam_params, head_gain.reshape(heads, 1, vd), qx, kn, vn, *([ck] * pages), *([cv] * pages))
    return out.transpose(0, 2, 1, 3).reshape(n, heads * vd)


def _attn_fused_kernel(pt_ref, lam_ref, fgain_ref, q_ref, kt_ref, v_ref, pgain_ref, qx_ref, kn_ref, vn_ref,
                       *rest, pages, heads, tq, hd, tn, lam_init):
    k_refs, v_refs = rest[:pages], rest[pages:2 * pages]
    o_ref, po_ref = rest[2 * pages:2 * pages + 2]
    scratch = rest[2 * pages + 2:]
    i = pl.program_id(2)
    _flash_body(i, lam_ref, fgain_ref, q_ref, kt_ref, v_ref, o_ref, *scratch[:6], tq=tq, hd=hd, lam_init=lam_init)
    _paged_body(i, i == pl.num_programs(2) - 1, lam_ref, pgain_ref, qx_ref, kn_ref, vn_ref, k_refs, v_refs,
                po_ref, *scratch[6:], heads=heads, tn=tn, lam_init=lam_init)


def _fused_attention_fits(batch, seq, heads, sample_batch, n_pages):
    tq = min(512, seq)
    nq = seq // tq
    return batch * heads == sample_batch and n_pages % nq == 0 and n_pages // nq in (8, 16)


def _fused_attention(q, kt, v, qs, k_new, v_new, cache_k, cache_v, layer, page_table, lam_params, head_gain,
                     batch, seq, tn, lam_init):
    n, nqk = q.shape
    heads, vd = head_gain.shape
    hd = nqk // (2 * heads)
    tq = min(512, seq)
    nq = seq // tq
    assert vd == LANES and 2 * hd == LANES and tq % LANES == 0
    ns, d = qs.shape
    n_pages = page_table.shape[1]
    page = cache_k.shape[2]
    pages = n_pages // nq
    rows = tn * 2 * heads
    qx, kn, vn, ck, cv = _paged_operands(qs, k_new, v_new, cache_k, cache_v, heads, vd, tn)
    new_keys = kn.shape[1]

    def page_spec(block_rows, block_cols, slot):
        return pl.BlockSpec((None, None, block_rows, block_cols),
                            lambda b, h, i, pt: (layer, pt[b * heads + h, i * pages + slot], 0, 0))

    per_seq = lambda b, h, i, pt: (b * heads + h, 0, 0)
    gain3 = head_gain.reshape(heads, 1, vd)
    grid_spec = pltpu.PrefetchScalarGridSpec(
        num_scalar_prefetch=1,
        grid=(batch, heads, nq),
        in_specs=[
            pl.BlockSpec(lam_params.shape, lambda b, h, i, pt: (0, 0)),
            pl.BlockSpec((1, 1, vd), lambda b, h, i, pt: (h, 0, 0)),
            pl.BlockSpec((tq, 2 * hd), lambda b, h, i, pt: (b * nq + i, h)),
            pl.BlockSpec((None, 2 * hd, seq), lambda b, h, i, pt: (b, h, 0)),
            pl.BlockSpec((seq, vd), lambda b, h, i, pt: (b, h)),
            pl.BlockSpec((heads, 1, vd), lambda b, h, i, pt: (0, 0, 0)),
            pl.BlockSpec((1, rows, d), per_seq),
            pl.BlockSpec((1, new_keys, d), per_seq),
            pl.BlockSpec((1, new_keys * heads, vd), per_seq),
        ] + [page_spec(d, page, s) for s in range(pages)]
        + [page_spec(page * heads, vd, s) for s in range(pages)],
        out_specs=[pl.BlockSpec((tq, vd), lambda b, h, i, pt: (b * nq + i, h)),
                   pl.BlockSpec((1, heads, tn, vd), lambda b, h, i, pt: (b * heads + h, 0, 0, 0))],
        scratch_shapes=[pltpu.VMEM((nq, 2 * hd, tq), BF16), pltpu.VMEM((nq, tq, vd), BF16),
                        pltpu.VMEM((2 * tq, 2 * hd), BF16), pltpu.VMEM((2 * tq, LANES), F32),
                        pltpu.VMEM((2 * tq, LANES), F32), pltpu.VMEM((2 * tq, vd), F32),
                        pltpu.VMEM((rows, LANES), F32), pltpu.VMEM((rows, LANES), F32),
                        pltpu.VMEM((rows, vd), F32)],
    )
    dp, ds = pl.pallas_call(
        functools.partial(_attn_fused_kernel, pages=pages, heads=heads, tq=tq, hd=hd, tn=tn, lam_init=lam_init),
        grid_spec=grid_spec,
        out_shape=[jax.ShapeDtypeStruct((n, heads * vd), BF16),
                   jax.ShapeDtypeStruct((ns // tn, heads, tn, vd), F32)],
        compiler_params=_params("parallel", "parallel", "arbitrary"),
        name="attn_fused",
    )(page_table, lam_params, gain3, q, kt, v, gain3, qx, kn, vn, *([ck] * pages), *([cv] * pages))
    return dp, ds.transpose(0, 2, 1, 3).reshape(ns, heads * vd)


def _post_kernel(x_ref, d_ref, wo_ref, g_ref, wg_ref, wu_ref, wd_ref, fg_ref, o_ref, *, chunks, final):
    x1 = x_ref[...] + _dot(d_ref[...].astype(BF16), wo_ref[...])
    h = _rms(x1, g_ref[...]).astype(BF16)
    acc = x1
    for c0, c1 in chunks:
        g = _dot(h, wg_ref[:, c0:c1])
        u = _dot(h, wu_ref[:, c0:c1])
        a = (g / (1.0 + jnp.exp(-g)) * u).astype(BF16)
        acc = acc + _dot(a, wd_ref[c0:c1, :])
    o_ref[...] = _rms(acc, fg_ref[...]) if final else acc


def _post(x2d, d2d, wo, ffn_gain, wg, wu, wd, final_gain, final):
    n, dm = x2d.shape
    f = wg.shape[1]
    tm = _row_tile(n)
    step = 1024
    chunks = tuple((c, min(c + step, f)) for c in range(0, f, step))
    row = lambda i: (i, 0)
    return pl.pallas_call(
        functools.partial(_post_kernel, chunks=chunks, final=final),
        grid=(n // tm,),
        in_specs=[pl.BlockSpec((tm, dm), row), pl.BlockSpec((tm, dm), row), _resident(wo.shape),
                  _resident((1, dm)), _resident(wg.shape), _resident(wu.shape), _resident(wd.shape),
                  _resident((1, dm))],
        out_specs=pl.BlockSpec((tm, dm), row),
        out_shape=jax.ShapeDtypeStruct((n, dm), F32),
        compiler_params=_params("parallel"),
        name="post_final" if final else "post",
    )(x2d, d2d, wo, ffn_gain.reshape(1, dm), wg, wu, wd, final_gain.reshape(1, dm))


def _log_gates(g, is_forget):
    ig = GATE_SOFTCAP * jnp.tanh(g / GATE_SOFTCAP)
    lf = jnp.minimum(g, 0.0) - jnp.log1p(jnp.exp(-jnp.abs(g)))
    return jnp.where(is_forget, lf, ig)


def _mlstm_proj_kernel(x_ref, g_ref, w_ref, wkt_ref, wgt_ref, bc_ref, br_ref,
                       q_ref, kt_ref, v_ref, og_ref, gc_ref, gr_ref, *, hk, hv, q_scale):
    h = _rms(x_ref[...], g_ref[...]).astype(BF16)
    q_ref[...] = _dot(h, w_ref[:, :hk]) * q_scale
    kt_ref[...] = _nt_dot(wkt_ref[...], h)
    v_ref[...] = _dot(h, w_ref[:, 2 * hk:2 * hk + hv])
    og_ref[...] = _dot(h, w_ref[:, 2 * hk + hv:])
    wgt = wgt_ref[...]
    ng = wgt.shape[0]
    gcol = lax.dot_general(h, wgt, (((1,), (1,)), ((), ())), preferred_element_type=F32) + bc_ref[...]
    grow = _nt_dot(wgt, h) + br_ref[...]
    gc_ref[...] = _log_gates(gcol, lax.broadcasted_iota(jnp.int32, gcol.shape, 1) >= ng // 2)
    gr_ref[...] = _log_gates(grow, lax.broadcasted_iota(jnp.int32, grow.shape, 0) >= ng // 2)


def _mlstm_proj(x2d, gain, w_main, w_k_t, w_gates_t, b_gates, hk, hv, q_scale):
    n, d = x2d.shape
    tm = _row_tile(n)
    ng = w_gates_t.shape[0]
    row = lambda i: (i, 0)
    col = lambda i: (0, i)
    return pl.pallas_call(
        functools.partial(_mlstm_proj_kernel, hk=hk, hv=hv, q_scale=q_scale),
        grid=(n // tm,),
        in_specs=[pl.BlockSpec((tm, d), row), _resident((1, d)), _resident(w_main.shape),
                  _resident(w_k_t.shape), _resident(w_gates_t.shape), _resident((1, ng)), _resident((ng, 1))],
        out_specs=[pl.BlockSpec((tm, hk), row), pl.BlockSpec((hk, tm), col), pl.BlockSpec((tm, hv), row),
                   pl.BlockSpec((tm, d), row), pl.BlockSpec((tm, ng), row), pl.BlockSpec((ng, tm), col)],
        out_shape=[jax.ShapeDtypeStruct((n, hk), F32), jax.ShapeDtypeStruct((hk, n), F32),
                   jax.ShapeDtypeStruct((n, hv), F32), jax.ShapeDtypeStruct((n, d), F32),
                   jax.ShapeDtypeStruct((n, ng), F32), jax.ShapeDtypeStruct((ng, n), F32)],
        compiler_params=_params("parallel"),
        name="mlstm_proj",
    )(x2d, gain.reshape(1, d), w_main, w_k_t, w_gates_t, b_gates.reshape(1, ng), b_gates.reshape(ng, 1))


def _split3(x):
    hi = x.astype(BF16)
    r1 = x - hi.astype(F32)
    mid = r1.astype(BF16)
    lo = (r1 - mid.astype(F32)).astype(BF16)
    return hi, mid, lo


def _mlstm_kernel(q_ref, kt_ref, v_ref, og_ref, gc_ref, gr_ref, gain_ref, s0_ref, m0_ref,
                  y_ref, s_out_ref, m_out_ref, s_sc, m_sc, *, heads):
    c = pl.program_id(1)
    L = q_ref.shape[0]
    dk = q_ref.shape[1] // heads
    dk2 = 2 * dk
    dv = v_ref.shape[1] // heads

    @pl.when(c == 0)
    def _():
        for pr in range(heads // 2):
            s_sc[pr] = jnp.concatenate([s0_ref[0, 2 * pr], s0_ref[0, 2 * pr + 1]], axis=0)
        m_sc[...] = m0_ref[0]

    tt = lax.broadcasted_iota(jnp.int32, (L, L), 0)
    ss = lax.broadcasted_iota(jnp.int32, (L, L), 1)
    causal = ss <= tt
    tril = jnp.where(causal, 1.0, 0.0).astype(BF16)
    triu = jnp.where(tt <= ss, 1.0, 0.0).astype(BF16)

    gc = gc_ref[...]
    gr = gr_ref[0]
    bcol = sum(_dot(tril, part) for part in _split3(gc))
    brow = sum(_dot(part, triu) for part in _split3(gr))
    lane = lax.broadcasted_iota(jnp.int32, (L, dk2), 1)
    sub = lax.broadcasted_iota(jnp.int32, (dk2, L), 0)
    one_col = jnp.where(lax.broadcasted_iota(jnp.int32, (L, dv), 1) == 0, 1.0, 0.0)
    last = c == pl.num_programs(1) - 1

    H = range(heads)
    P = range(heads // 2)
    qs = [q_ref[:, pr * dk2:(pr + 1) * dk2].astype(BF16) for pr in P]
    kts = [kt_ref[0, pr * dk2:(pr + 1) * dk2, :].astype(BF16) for pr in P]
    i_col = [gc[:, h:h + 1] for h in H]
    b_col = [bcol[:, heads + h:heads + h + 1] for h in H]
    i_row = [gr[h:h + 1, :] for h in H]
    b_row = [brow[heads + h:heads + h + 1, :] for h in H]
    b_last = [b[:, L - 1:L] for b in b_row]

    u = [jnp.where(causal, i_row[h] - b_row[h], -jnp.inf) for h in H]
    cm = [jnp.max(x, axis=-1, keepdims=True) for x in u]
    e = [jnp.exp(u[h] - cm[h]) for h in H]
    a = [b_col[h] + cm[h] for h in H]
    qz = [jnp.where((lane >= (h % 2) * dk) & (lane < (h % 2 + 1) * dk), qs[h // 2], jnp.zeros_like(qs[0]))
          for h in H]
    qk = [_dot(qz[h], kts[h // 2]) for h in H]
    g = [(qk[h] * e[h]).astype(BF16) for h in H]
    vx = [jnp.concatenate([v_ref[:, h * dv:(h + 1) * dv], one_col], axis=1) for h in H]
    intra = [_dot(g[h], vx[h].astype(BF16)) for h in H]

    m_loc = [jnp.max(b_last[h] - b_row[h] + i_row[h], axis=-1, keepdims=True) for h in H]
    ws_col = [jnp.exp(b_last[h] - b_col[h] + i_col[h] - m_loc[h]) for h in H]
    ktz = [jnp.where((sub >= (h % 2) * dk) & (sub < (h % 2 + 1) * dk), kts[h // 2], jnp.zeros_like(kts[0]))
           for h in H]
    upd = [_dot(ktz[h], (ws_col[h] * vx[h]).astype(BF16)) for h in H]

    state = [s_sc[pr] for pr in P]
    state_bf = [x.astype(BF16) for x in state]
    inter_c = [_dot(qz[h], state_bf[h // 2]) for h in H]
    m_prev = [m_sc[h][:, :1] for h in H]
    inter = [b_col[h] + m_prev[h] for h in H]
    mt = [jnp.maximum(inter[h], a[h]) for h in H]
    nd = [jnp.exp(inter[h] - mt[h]) * inter_c[h] + jnp.exp(a[h] - mt[h]) * intra[h] for h in H]
    inv = [1.0 / jnp.maximum(jnp.abs(nd[h][:, dv:dv + 1]), jnp.exp(-mt[h])) for h in H]
    msq = [jnp.mean(nd[h][:, :dv] * nd[h][:, :dv], axis=-1, keepdims=True) for h in H]
    scale = [inv[h] * lax.rsqrt(msq[h] * inv[h] * inv[h] + RMS_EPS) for h in H]
    sig = [1.0 + jnp.exp(-og_ref[:, h * dv:(h + 1) * dv]) for h in H]
    for h in H:
        y_ref[:, h * dv:(h + 1) * dv] = (nd[h][:, :dv] * scale[h] * gain_ref[h] / sig[h]).astype(y_ref.dtype)

    m_new = [jnp.maximum(b_last[h] + m_prev[h], m_loc[h]) for h in H]
    gain_new = [jnp.exp(m_loc[h] - m_new[h]) for h in H]
    decay = [jnp.exp(b_last[h] + m_prev[h] - m_new[h]) for h in H]
    row_id = lax.broadcasted_iota(jnp.int32, state[0].shape, 0)
    for pr in P:
        h0, h1 = 2 * pr, 2 * pr + 1
        s_sc[pr] = (gain_new[h0] * upd[h0] + gain_new[h1] * upd[h1]
                    + jnp.where(row_id < dk, decay[h0], decay[h1]) * state[pr])
    for h in H:
        m_sc[h] = jnp.broadcast_to(m_new[h], m_sc.shape[1:])

    @pl.when(last)
    def _():
        for pr in range(heads // 2):
            s_out_ref[0, 2 * pr] = s_sc[pr, :dk]
            s_out_ref[0, 2 * pr + 1] = s_sc[pr, dk:]
        m_out_ref[0] = m_sc[...]


def _mlstm(q, kt, v, og, gc, gr, head_gain, s0, m0, batch, nc):
    n, hk = q.shape
    heads, dv = head_gain.shape
    dk = hk // heads
    L = n // (batch * nc)
    ng = gc.shape[1]
    blk = lambda b, c: (b * nc + c, 0)
    st = lambda b, c: (b, 0, 0, 0)
    if kt.shape[0] == 1:
        lanes_blk = lambda b, c: (0, 0, b * nc + c)
    else:
        lanes_blk = lambda b, c: (b * nc + c, 0, 0)
    y, s_out, m_out = pl.pallas_call(
        functools.partial(_mlstm_kernel, heads=heads),
        grid=(batch, nc),
        in_specs=[
            pl.BlockSpec((L, hk), blk), pl.BlockSpec((1, hk, L), lanes_blk),
            pl.BlockSpec((L, heads * dv), blk), pl.BlockSpec((L, heads * dv), blk),
            pl.BlockSpec((L, ng), blk),
            pl.BlockSpec((1, ng, L), lanes_blk),
            pl.BlockSpec((heads, 1, dv), lambda b, c: (0, 0, 0)),
            pl.BlockSpec((1, heads, dk, 2 * dv), st),
            pl.BlockSpec((1, heads, 1, LANES), st),
        ],
        out_specs=[pl.BlockSpec((L, heads * dv), blk), pl.BlockSpec((1, heads, dk, 2 * dv), st),
                   pl.BlockSpec((1, heads, 1, LANES), st)],
        out_shape=[jax.ShapeDtypeStruct((n, heads * dv), BF16),
                   jax.ShapeDtypeStruct((batch, heads, dk, 2 * dv), F32),
                   jax.ShapeDtypeStruct((batch, heads, 1, LANES), F32)],
        scratch_shapes=[pltpu.VMEM((heads // 2, 2 * dk, 2 * dv), F32), pltpu.VMEM((heads, 1, LANES), F32)],
        compiler_params=_params("parallel", "arbitrary"),
        name="mlstm",
    )(q, kt, v, og, gc, gr, head_gain.reshape(heads, 1, dv), s0, m0)
    return y, s_out[..., :dv], s_out[..., dv], m_out[:, :, 0, 0]


def _pack_state(C, nvec, m):
    dv = C.shape[-1]
    s0 = jnp.concatenate([C, nvec[..., None], jnp.zeros(C.shape[:-1] + (dv - 1,), F32)], axis=-1)
    return s0, jnp.broadcast_to(m[..., None, None], m.shape + (1, LANES))


def _pad_chunks(a, batch, t, L, value=0.0):
    w = a.shape[-1]
    return jnp.pad(a.reshape(batch, t, w), ((0, 0), (0, L - t), (0, 0)), constant_values=value).reshape(batch * L, w)


def _mlstm_layer(x2d, batch, t, norm_gain, w_main, w_k_t, w_gates_t, b_gates, head_gain, C0, n0, m0):
    heads, dv = head_gain.shape
    ng = 2 * heads
    hk = w_k_t.shape[0]
    dk = hk // heads
    q, kt, v, og, gc, gr = _mlstm_proj(x2d, norm_gain, w_main, w_k_t, w_gates_t, b_gates, hk, heads * dv,
                                       dk ** -0.5)
    s0, m0p = _pack_state(C0, n0, m0)
    if t % ML_CHUNK == 0:
        y, C, nvec, m = _mlstm(q, kt[None], v, og, gc, gr[None], head_gain, s0, m0p, batch, t // ML_CHUNK)
        return y, C, nvec, m
    L = -(-t // SHORT_CHUNK) * SHORT_CHUNK
    pad = lambda a: _pad_chunks(a, batch, t, L)
    gate_pad = jnp.concatenate([jnp.full((heads,), NEG_BIG, F32), jnp.zeros((heads,), F32)])
    gcp = jnp.where((jnp.arange(batch * L) % L < t)[:, None], pad(gc), gate_pad[None, :])
    grp = gcp.reshape(batch, L, ng).transpose(0, 2, 1)
    ktp = jnp.pad(kt.reshape(hk, batch, t).transpose(1, 0, 2), ((0, 0), (0, 0), (0, L - t)))
    y, C, nvec, m = _mlstm(pad(q), ktp, pad(v), pad(og), gcp, grp, head_gain, s0, m0p, batch, 1)
    y = y.reshape(batch, L, -1)[:, :t].reshape(batch * t, -1)
    return y, C, nvec, m


def kernel(x_prompt, x_sample, cache_k, cache_v, page_table, state_C, state_n, state_m, attn_norm, w_attn_qkv, attn_lambda, attn_head_gain, w_attn_out, mlstm_norm, w_mlstm_in, b_mlstm_gates, mlstm_head_gain, w_mlstm_out, ffn_norm, w_ffn_gate, w_ffn_up, w_ffn_down, final_norm):
    bp, tp, dm = x_prompt.shape
    bs, ts, _ = x_sample.shape
    depth = ffn_norm.shape[0]
    n_mixers = 2
    da_heads, da_vd = attn_head_gain.shape[1:]
    ml_heads, ml_dv = mlstm_head_gain.shape[1:]
    nqk = (w_attn_qkv.shape[2] - da_heads * da_vd) // 2
    da_hd = nqk // (2 * da_heads)
    past_len = page_table.shape[1] * cache_k.shape[2]

    tab_p, tab_pt = _rope_tables(jnp.arange(tp, dtype=F32))
    tab_s = tuple(jnp.tile(a, (bs, 1)) for a in _rope_tables(past_len + jnp.arange(ts, dtype=F32))[0])

    xp = x_prompt.reshape(bp * tp, dm)
    xs = x_sample.reshape(bs * ts, dm)
    k_p, v_p, k_s, v_s = [], [], [], []
    c_p, n_p, m_p, c_s, n_s, m_s = [], [], [], [], [], []
    for i in range(depth):
        final = i == depth - 1
        if i % n_mixers == 0:
            a = i // n_mixers
            lam_init = 0.8 - 0.6 * math.exp(-0.3 * i)
            q_scale = da_hd ** -0.5 * LOG2E
            w_qkv = w_attn_qkv[a].astype(BF16)
            w_out = w_attn_out[a].astype(BF16)
            wkt = w_attn_qkv[a][:, nqk:2 * nqk].T.astype(BF16)
            q, kt, v = _attn_proj_kt(xp, attn_norm[a], w_qkv, wkt, tab_p, tab_pt, bp, nqk, da_hd,
                                     da_heads * da_vd, q_scale)
            qs, k, vs = _attn_proj(xs, attn_norm[a], w_qkv, tab_s, nqk, da_heads * da_vd, q_scale)
            if _fused_attention_fits(bp, tp, da_heads, bs, page_table.shape[1]):
                dp, ds = _fused_attention(q, kt, v, qs, k, vs, cache_k, cache_v, a, page_table, attn_lambda[a],
                                          attn_head_gain[a], bp, tp, ts, lam_init)
            else:
                dp = _flash_attention(q, kt, v, attn_lambda[a], attn_head_gain[a], bp, tp, lam_init)
                ds = _paged_attention(qs, k, vs, cache_k, cache_v, a, page_table, attn_lambda[a],
                                      attn_head_gain[a], ts, lam_init)
            k_p.append(kt.reshape(bp, 2 * da_heads, da_hd, tp).transpose(0, 3, 1, 2))
            v_p.append(v.reshape(bp, tp, da_heads, da_vd))
            k_s.append(k.reshape(bs, ts, 2 * da_heads, da_hd))
            v_s.append(vs.reshape(bs, ts, da_heads, da_vd))
        else:
            j = i // n_mixers
            hv = ml_heads * ml_dv
            w_in = w_mlstm_in[j]
            w_main = w_in[:, :w_in.shape[1] - 2 * ml_heads].astype(BF16)
            w_gates_t = w_in[:, w_in.shape[1] - 2 * ml_heads:].T.astype(BF16)
            w_out = w_mlstm_out[j].astype(BF16)
            hk = (w_main.shape[1] - hv - dm) // 2
            dk = hk // ml_heads
            w_k_t = w_in[:, hk:2 * hk].T.astype(BF16)
            zero = (jnp.zeros((bp, ml_heads, dk, ml_dv), F32), jnp.zeros((bp, ml_heads, dk), F32),
                    jnp.zeros((bp, ml_heads), F32))
            dp, C, nv, m = _mlstm_layer(xp, bp, tp, mlstm_norm[j], w_main, w_k_t, w_gates_t, b_mlstm_gates[j],
                                        mlstm_head_gain[j], *zero)
            c_p.append(C); n_p.append(nv); m_p.append(m)
            ds, C, nv, m = _mlstm_layer(xs, bs, ts, mlstm_norm[j], w_main, w_k_t, w_gates_t, b_mlstm_gates[j],
                                        mlstm_head_gain[j], state_C[j], state_n[j], state_m[j])
            c_s.append(C); n_s.append(nv); m_s.append(m)
        wg, wu, wd = (w[i].astype(BF16) for w in (w_ffn_gate, w_ffn_up, w_ffn_down))
        xp = _post(xp, dp, w_out, ffn_norm[i], wg, wu, wd, final_norm, final)
        xs = _post(xs, ds, w_out, ffn_norm[i], wg, wu, wd, final_norm, final)
    return (xp.reshape(bp, tp, dm), xs.reshape(bs, ts, dm),
            jnp.stack(k_p), jnp.stack(v_p), jnp.stack(c_p), jnp.stack(n_p), jnp.stack(m_p),
            jnp.stack(k_s), jnp.stack(v_s), jnp.stack(c_s), jnp.stack(n_s), jnp.stack(m_s))
```

```python
import functools
import math

import jax
import jax.numpy as jnp
from jax import lax
from jax.experimental import pallas as pl
from jax.experimental.pallas import tpu as pltpu

RMS_EPS = 1e-6
ROPE_THETA = 10000.0
GATE_SOFTCAP = 15.0
ML_CHUNK = 128
SHORT_CHUNK = 16
LANES = 128
NEG_BIG = -1e30
LOG2E = math.log2(math.e)
VMEM_LIMIT_BYTES = 56 * 1024 * 1024

F32 = jnp.float32
BF16 = jnp.bfloat16


def _params(*sem):
    return pltpu.CompilerParams(dimension_semantics=sem, vmem_limit_bytes=VMEM_LIMIT_BYTES)


def _resident(shape):
    nd = len(shape)
    return pl.BlockSpec(shape, lambda *_: (0,) * nd, pipeline_mode=pl.Buffered(1))


def _row_tile(n):
    return 512 if n % 512 == 0 else n


def _rms(x, gain):
    return x * lax.rsqrt(jnp.mean(x * x, axis=-1, keepdims=True) + RMS_EPS) * gain


def _nt_dot(a, b):
    return lax.dot_general(a, b, (((1,), (1,)), ((), ())), preferred_element_type=F32)


def _dot(a, b):
    return jnp.dot(a, b, preferred_element_type=F32)


def _rope_tables(pos):
    half = 32
    inv = ROPE_THETA ** (-jnp.arange(half, dtype=F32) * 2.0 / (2 * half))
    ang = pos[:, None] * inv[None, :]
    cos, sin = jnp.cos(ang), jnp.sin(ang)
    zero = jnp.zeros_like(sin)
    cos_t = jnp.tile(cos, (1, 4))
    sin_lo = jnp.tile(jnp.concatenate([-sin, zero], axis=1), (1, 2))
    sin_hi = jnp.tile(jnp.concatenate([zero, sin], axis=1), (1, 2))
    return (cos_t, sin_lo, sin_hi), (cos.T, sin.T)


def _rope_lanes(y, o_ref, cos, slo, shi, scale):
    for c in range(y.shape[1] // LANES):
        yc = y[:, c * LANES:(c + 1) * LANES]
        r = yc * cos + pltpu.roll(yc, LANES - 32, 1) * slo + pltpu.roll(yc, 32, 1) * shi
        o_ref[:, c * LANES:(c + 1) * LANES] = r if scale == 1.0 else r * scale


def _attn_proj_kernel(x_ref, g_ref, w_ref, cos_ref, slo_ref, shi_ref, q_ref, k_ref, v_ref, *, nqk, q_scale):
    h = _rms(x_ref[...], g_ref[...]).astype(BF16)
    cos, slo, shi = cos_ref[...], slo_ref[...], shi_ref[...]
    _rope_lanes(_dot(h, w_ref[:, :nqk]), q_ref, cos, slo, shi, q_scale)
    _rope_lanes(_dot(h, w_ref[:, nqk:2 * nqk]), k_ref, cos, slo, shi, 1.0)
    v_ref[...] = _dot(h, w_ref[:, 2 * nqk:])


def _attn_proj_kt_kernel(x_ref, g_ref, w_ref, wkt_ref, cos_ref, slo_ref, shi_ref, cst_ref, snt_ref,
                         q_ref, kt_ref, v_ref, *, nqk, hd, q_scale):
    h = _rms(x_ref[...], g_ref[...]).astype(BF16)
    _rope_lanes(_dot(h, w_ref[:, :nqk]), q_ref, cos_ref[...], slo_ref[...], shi_ref[...], q_scale)
    kt = _nt_dot(wkt_ref[...], h)
    cs, sn = cst_ref[...], snt_ref[...]
    half = hd // 2
    for mp in range(nqk // hd):
        x1 = kt[mp * hd:mp * hd + half]
        x2 = kt[mp * hd + half:(mp + 1) * hd]
        kt_ref[mp * hd:mp * hd + half, :] = x1 * cs - x2 * sn
        kt_ref[mp * hd + half:(mp + 1) * hd, :] = x2 * cs + x1 * sn
    v_ref[...] = _dot(h, w_ref[:, 2 * nqk:])


def _attn_proj(x2d, gain, w_bf16, tables, nqk, nv, q_scale):
    n, d = x2d.shape
    tm = _row_tile(n)
    period = tables[0].shape[0] // tm
    row = lambda i: (i, 0)
    tab = pl.BlockSpec((tm, LANES), lambda i: (i % period, 0))
    return pl.pallas_call(
        functools.partial(_attn_proj_kernel, nqk=nqk, q_scale=q_scale),
        grid=(n // tm,),
        in_specs=[pl.BlockSpec((tm, d), row), _resident((1, d)), _resident(w_bf16.shape), tab, tab, tab],
        out_specs=[pl.BlockSpec((tm, nqk), row), pl.BlockSpec((tm, nqk), row), pl.BlockSpec((tm, nv), row)],
        out_shape=[jax.ShapeDtypeStruct((n, nqk), F32), jax.ShapeDtypeStruct((n, nqk), F32),
                   jax.ShapeDtypeStruct((n, nv), F32)],
        compiler_params=_params("parallel"),
        name="attn_proj",
    )(x2d, gain.reshape(1, d), w_bf16, *tables)


def _attn_proj_kt(x2d, gain, w_bf16, wkt_bf16, tables, tables_t, batch, nqk, hd, nv, q_scale):
    n, d = x2d.shape
    seq = n // batch
    tm = _row_tile(seq)
    period = seq // tm
    row = lambda i: (i, 0)
    tab = pl.BlockSpec((tm, LANES), lambda i: (i % period, 0))
    tab_t = pl.BlockSpec((hd // 2, tm), lambda i: (0, i % period))
    return pl.pallas_call(
        functools.partial(_attn_proj_kt_kernel, nqk=nqk, hd=hd, q_scale=q_scale),
        grid=(n // tm,),
        in_specs=[pl.BlockSpec((tm, d), row), _resident((1, d)), _resident(w_bf16.shape),
                  _resident(wkt_bf16.shape), tab, tab, tab, tab_t, tab_t],
        out_specs=[pl.BlockSpec((tm, nqk), row),
                   pl.BlockSpec((None, nqk, tm), lambda i: (i // period, 0, i % period)),
                   pl.BlockSpec((tm, nv), row)],
        out_shape=[jax.ShapeDtypeStruct((n, nqk), F32), jax.ShapeDtypeStruct((batch, nqk, seq), F32),
                   jax.ShapeDtypeStruct((n, nv), F32)],
        compiler_params=_params("parallel"),
        name="attn_proj_kt",
    )(x2d, gain.reshape(1, d), w_bf16, wkt_bf16, *tables, *tables_t)


def _lambda(lam_ref, lam_init):
    lp = lam_ref[...]
    a = jnp.sum(lp[0:1, :] * lp[1:2, :], axis=-1, keepdims=True)
    b = jnp.sum(lp[2:3, :] * lp[3:4, :], axis=-1, keepdims=True)
    return jnp.exp(a) - jnp.exp(b) + lam_init


def _flash_kernel(*refs, **static):
    _flash_body(pl.program_id(2), *refs, **static)


def _flash_body(i, lam_ref, gain_ref, q_ref, kt_ref, v_ref, o_ref, k_sc, v_sc, qq_sc, m_sc, l_sc, acc_sc,
                *, tq, hd, lam_init):
    @pl.when(i == 0)
    def _():
        for c in range(k_sc.shape[0]):
            k_sc[c] = kt_ref[:, c * tq:(c + 1) * tq].astype(BF16)
            v_sc[c] = v_ref[c * tq:(c + 1) * tq, :].astype(BF16)

    q = q_ref[...].astype(BF16)
    lane = lax.broadcasted_iota(jnp.int32, q.shape, 1)
    zero = jnp.zeros_like(q)
    qq_sc[:tq] = jnp.where(lane < hd, q, zero)
    qq_sc[tq:] = jnp.where(lane >= hd, q, zero)
    m_sc[...] = jnp.full(m_sc.shape, -jnp.inf, F32)
    l_sc[...] = jnp.zeros(l_sc.shape, F32)
    acc_sc[...] = jnp.zeros(acc_sc.shape, F32)

    def update(j, diagonal):
        s = _dot(qq_sc[...], k_sc[j])
        if diagonal:
            r = lax.broadcasted_iota(jnp.int32, (2 * tq, tq), 0)
            c = lax.broadcasted_iota(jnp.int32, (2 * tq, tq), 1)
            s = jnp.where(c <= jnp.where(r >= tq, r - tq, r), s, -jnp.inf)
        m_prev = m_sc[...]
        m_new = jnp.maximum(m_prev, jnp.max(s, axis=-1, keepdims=True))
        alpha = jnp.exp2(m_prev - m_new)
        p = jnp.exp2(s - jnp.concatenate([m_new] * (tq // LANES), axis=1))
        acc_sc[...] = acc_sc[...] * alpha + _dot(p.astype(BF16), v_sc[j])
        l_sc[...] = l_sc[...] * alpha + jnp.sum(p, axis=-1, keepdims=True)
        m_sc[...] = m_new

    def pair(jj, carry):
        update(2 * jj, False)
        update(2 * jj + 1, False)
        return carry

    lax.fori_loop(0, i // 2, pair, 0)

    @pl.when(i % 2 == 1)
    def _():
        update(i - 1, False)

    update(i, True)

    o = acc_sc[...] / l_sc[...]
    d = o[:tq] - _lambda(lam_ref, lam_init) * o[tq:]
    d = d * lax.rsqrt(jnp.mean(d * d, axis=-1, keepdims=True) + RMS_EPS)
    o_ref[...] = (d * gain_ref[0] * (1.0 - lam_init)).astype(o_ref.dtype)


def _flash_attention(q, kt, v, lam_params, head_gain, batch, seq, lam_init):
    n, nqk = q.shape
    heads, vd = head_gain.shape
    hd = nqk // (2 * heads)
    tq = min(512, seq)
    nq = seq // tq
    assert vd == LANES and 2 * hd == LANES and tq % LANES == 0
    return pl.pallas_call(
        functools.partial(_flash_kernel, tq=tq, hd=hd, lam_init=lam_init),
        grid=(batch, heads, nq),
        in_specs=[
            _resident(lam_params.shape),
            pl.BlockSpec((1, 1, vd), lambda b, h, i: (h, 0, 0)),
            pl.BlockSpec((tq, 2 * hd), lambda b, h, i: (b * nq + i, h)),
            pl.BlockSpec((None, 2 * hd, seq), lambda b, h, i: (b, h, 0)),
            pl.BlockSpec((seq, vd), lambda b, h, i: (b, h)),
        ],
        out_specs=pl.BlockSpec((tq, vd), lambda b, h, i: (b * nq + i, h)),
        out_shape=jax.ShapeDtypeStruct((n, heads * vd), BF16),
        scratch_shapes=[pltpu.VMEM((nq, 2 * hd, tq), BF16), pltpu.VMEM((nq, tq, vd), BF16),
                        pltpu.VMEM((2 * tq, 2 * hd), BF16), pltpu.VMEM((2 * tq, LANES), F32),
                        pltpu.VMEM((2 * tq, LANES), F32), pltpu.VMEM((2 * tq, vd), F32)],
        compiler_params=_params("parallel", "parallel", "arbitrary"),
        name="flash_attn",
    )(lam_params, head_gain.reshape(heads, 1, vd), q, kt, v)


def _paged_kernel(pt_ref, lam_ref, gain_ref, qx_ref, kn_ref, vn_ref, *rest, pages, heads, tn, lam_init):
    k_refs, v_refs = rest[:pages], rest[pages:2 * pages]
    o_ref, m_sc, l_sc, acc_sc = rest[2 * pages:]
    j = pl.program_id(1)
    _paged_body(j, j == pl.num_programs(1) - 1, lam_ref, gain_ref, qx_ref, kn_ref, vn_ref, k_refs, v_refs,
                o_ref, m_sc, l_sc, acc_sc, heads=heads, tn=tn, lam_init=lam_init)


def _paged_body(j, is_last, lam_ref, gain_ref, qx_ref, kn_ref, vn_ref, k_refs, v_refs,
                o_ref, m_sc, l_sc, acc_sc, *, heads, tn, lam_init):
    pages = len(k_refs)
    qx = qx_ref[0]
    rows = qx.shape[0]
    rph = rows // heads
    keys = kn_ref.shape[1]

    @pl.when(j == 0)
    def _():
        m_sc[...] = jnp.full(m_sc.shape, -jnp.inf, F32)
        l_sc[...] = jnp.zeros(l_sc.shape, F32)
        acc_sc[...] = jnp.zeros(acc_sc.shape, F32)

    def update(ss, v_list):
        m = m_sc[:, :1]
        tops = [jnp.max(s, axis=-1, keepdims=True) for s in ss]
        m_new = functools.reduce(jnp.maximum, tops, m)
        alpha = jnp.exp2(m - m_new)
        ps = [jnp.exp2(s - m_new) for s in ss]
        sums = [jnp.sum(p, axis=-1, keepdims=True) for p in ps]
        pbs = [p.astype(BF16) for p in ps]
        outs = [jnp.concatenate(
            [_dot(pb[h * rph:(h + 1) * rph], v_ref[pl.ds(h, pb.shape[1], stride=heads), :].astype(BF16))
             for h in range(heads)], axis=0) for pb, v_ref in zip(pbs, v_list)]
        m_sc[...] = jnp.broadcast_to(m_new, m_sc.shape)
        l_sc[...] = jnp.broadcast_to(l_sc[:, :1] * alpha + functools.reduce(jnp.add, sums), l_sc.shape)
        acc_sc[...] = acc_sc[...] * alpha + functools.reduce(jnp.add, outs)

    update([_dot(qx, r[...].astype(BF16)) for r in k_refs], v_refs)

    @pl.when(is_last)
    def _():
        t = lax.broadcasted_iota(jnp.int32, (rows, keys), 0) % tn
        c = lax.broadcasted_iota(jnp.int32, (rows, keys), 1)
        s_new = _nt_dot(qx, kn_ref[0].astype(BF16))
        update([jnp.where(c <= t, s_new, -jnp.inf)], [vn_ref.at[0]])

        o = (acc_sc[...] / l_sc[:, :1]).reshape(heads, rph, acc_sc.shape[1])
        d = o[:, :tn, :] - _lambda(lam_ref, lam_init) * o[:, tn:, :]
        d = d * lax.rsqrt(jnp.mean(d * d, axis=-1, keepdims=True) + RMS_EPS)
        o_ref[0] = d * gain_ref[...] * (1.0 - lam_init)


def _paged_operands(q, k_new, v_new, cache_k, cache_v, heads, vd, tn):
    n, d = q.shape
    batch = n // tn
    hd = d // (2 * heads)
    page = cache_k.shape[2]
    rows = tn * 2 * heads

    r = jnp.arange(rows)
    owner = r // tn
    lane_map = jnp.arange(d) // hd
    qx = jnp.where(owner[None, :, None] == lane_map[None, None, :],
                   jnp.tile(q.reshape(batch, tn, d), (1, 2 * heads, 1)), 0.0).astype(BF16)

    new_keys = -(-tn // SHORT_CHUNK) * SHORT_CHUNK
    kn = jnp.pad(k_new.reshape(batch, tn, d), ((0, 0), (0, new_keys - tn), (0, 0)))
    vn = jnp.pad(v_new.reshape(batch, tn, heads, vd), ((0, 0), (0, new_keys - tn), (0, 0), (0, 0)))
    vn = vn.reshape(batch, new_keys * heads, vd)

    nl, n_pool = cache_k.shape[:2]
    ck = cache_k.transpose(0, 1, 3, 4, 2).reshape(nl, n_pool, d, page)
    cv = cache_v.reshape(nl, n_pool, page * heads, vd)
    return qx, kn, vn, ck, cv


def _paged_attention(q, k_new, v_new, cache_k, cache_v, layer, page_table, lam_params, head_gain, tn, lam_init):
    n, d = q.shape
    batch = n // tn
    heads, vd = head_gain.shape
    n_pages = page_table.shape[1]
    page = cache_k.shape[2]
    pages = next(p for p in (16, 8, 4, 2, 1) if n_pages % p == 0)
    rows = tn * 2 * heads
    qx, kn, vn, ck, cv = _paged_operands(q, k_new, v_new, cache_k, cache_v, heads, vd, tn)
    new_keys = kn.shape[1]

    def page_spec(block_rows, block_cols, slot):
        return pl.BlockSpec((None, None, block_rows, block_cols),
                            lambda b, j, pt: (layer, pt[b, j * pages + slot], 0, 0))

    per_b = lambda b, j, pt: (b, 0, 0)
    grid_spec = pltpu.PrefetchScalarGridSpec(
        num_scalar_prefetch=1,
        grid=(batch, n_pages // pages),
        in_specs=[
            pl.BlockSpec(lam_params.shape, lambda b, j, pt: (0, 0)),
            pl.BlockSpec((heads, 1, vd), lambda b, j, pt: (0, 0, 0)),
            pl.BlockSpec((1, rows, d), per_b),
            pl.BlockSpec((1, new_keys, d), per_b),
            pl.BlockSpec((1, new_keys * heads, vd), per_b),
        ] + [page_spec(d, page, s) for s in range(pages)]
        + [page_spec(page * heads, vd, s) for s in range(pages)],
        out_specs=pl.BlockSpec((1, heads, tn, vd), lambda b, j, pt: (b, 0, 0, 0)),
        scratch_shapes=[pltpu.VMEM((rows, LANES), F32), pltpu.VMEM((rows, LANES), F32),
                        pltpu.VMEM((rows, vd), F32)],
    )
    out = pl.pallas_call(
        functools.partial(_paged_kernel, pages=pages, heads=heads, tn=tn, lam_init=lam_init),
        grid_spec=grid_spec,
        out_shape=jax.ShapeDtypeStruct((batch, heads, tn, vd), F32),
        compiler_params=_params("parallel", "arbitrary"),
        name="paged_attn",
    )(page_table, lam_params, head_gain.reshape(heads, 1, vd), qx, kn, vn, *([ck] * pages), *([cv] * pages))
    return out.transpose(0, 2, 1, 3).reshape(n, heads * vd)


def _attn_fused_kernel(pt_ref, lam_ref, fgain_ref, q_ref, kt_ref, v_ref, pgain_ref, qx_ref, kn_ref, vn_ref,
                       *rest, pages, heads, tq, hd, tn, lam_init):
    k_refs, v_refs = rest[:pages], rest[pages:2 * pages]
    o_ref, po_ref = rest[2 * pages:2 * pages + 2]
    scratch = rest[2 * pages + 2:]
    i = pl.program_id(2)
    _flash_body(i, lam_ref, fgain_ref, q_ref, kt_ref, v_ref, o_ref, *scratch[:6], tq=tq, hd=hd, lam_init=lam_init)
    _paged_body(i, i == pl.num_programs(2) - 1, lam_ref, pgain_ref, qx_ref, kn_ref, vn_ref, k_refs, v_refs,
                po_ref, *scratch[6:], heads=heads, tn=tn, lam_init=lam_init)


def _fused_attention_fits(batch, seq, heads, sample_batch, n_pages):
    tq = min(512, seq)
    nq = seq // tq
    return batch * heads == sample_batch and n_pages % nq == 0 and n_pages // nq in (8, 16)


def _fused_attention(q, kt, v, qs, k_new, v_new, cache_k, cache_v, layer, page_table, lam_params, head_gain,
                     batch, seq, tn, lam_init):
    n, nqk = q.shape
    heads, vd = head_gain.shape
    hd = nqk // (2 * heads)
    tq = min(512, seq)
    nq = seq // tq
    assert vd == LANES and 2 * hd == LANES and tq % LANES == 0
    ns, d = qs.shape
    n_pages = page_table.shape[1]
    page = cache_k.shape[2]
    pages = n_pages // nq
    rows = tn * 2 * heads
    qx, kn, vn, ck, cv = _paged_operands(qs, k_new, v_new, cache_k, cache_v, heads, vd, tn)
    new_keys = kn.shape[1]

    def page_spec(block_rows, block_cols, slot):
        return pl.BlockSpec((None, None, block_rows, block_cols),
                            lambda b, h, i, pt: (layer, pt[b * heads + h, i * pages + slot], 0, 0))

    per_seq = lambda b, h, i, pt: (b * heads + h, 0, 0)
    gain3 = head_gain.reshape(heads, 1, vd)
    grid_spec = pltpu.PrefetchScalarGridSpec(
        num_scalar_prefetch=1,
        grid=(batch, heads, nq),
        in_specs=[
            pl.BlockSpec(lam_params.shape, lambda b, h, i, pt: (0, 0)),
            pl.BlockSpec((1, 1, vd), lambda b, h, i, pt: (h, 0, 0)),
            pl.BlockSpec((tq, 2 * hd), lambda b, h, i, pt: (b * nq + i, h)),
            pl.BlockSpec((None, 2 * hd, seq), lambda b, h, i, pt: (b, h, 0)),
            pl.BlockSpec((seq, vd), lambda b, h, i, pt: (b, h)),
            pl.BlockSpec((heads, 1, vd), lambda b, h, i, pt: (0, 0, 0)),
            pl.BlockSpec((1, rows, d), per_seq),
            pl.BlockSpec((1, new_keys, d), per_seq),
            pl.BlockSpec((1, new_keys * heads, vd), per_seq),
        ] + [page_spec(d, page, s) for s in range(pages)]
        + [page_spec(page * heads, vd, s) for s in range(pages)],
        out_specs=[pl.BlockSpec((tq, vd), lambda b, h, i, pt: (b * nq + i, h)),
                   pl.BlockSpec((1, heads, tn, vd), lambda b, h, i, pt: (b * heads + h, 0, 0, 0))],
        scratch_shapes=[pltpu.VMEM((nq, 2 * hd, tq), BF16), pltpu.VMEM((nq, tq, vd), BF16),
                        pltpu.VMEM((2 * tq, 2 * hd), BF16), pltpu.VMEM((2 * tq, LANES), F32),
                        pltpu.VMEM((2 * tq, LANES), F32), pltpu.VMEM((2 * tq, vd), F32),
                        pltpu.VMEM((rows, LANES), F32), pltpu.VMEM((rows, LANES), F32),
                        pltpu.VMEM((rows, vd), F32)],
    )
    dp, ds = pl.pallas_call(
        functools.partial(_attn_fused_kernel, pages=pages, heads=heads, tq=tq, hd=hd, tn=tn, lam_init=lam_init),
        grid_spec=grid_spec,
        out_shape=[jax.ShapeDtypeStruct((n, heads * vd), BF16),
                   jax.ShapeDtypeStruct((ns // tn, heads, tn, vd), F32)],
        compiler_params=_params("parallel", "parallel", "arbitrary"),
        name="attn_fused",
    )(page_table, lam_params, gain3, q, kt, v, gain3, qx, kn, vn, *([ck] * pages), *([cv] * pages))
    return dp, ds.transpose(0, 2, 1, 3).reshape(ns, heads * vd)


def _post_kernel(x_ref, d_ref, wo_ref, g_ref, wg_ref, wu_ref, wd_ref, fg_ref, o_ref, *, chunks, final):
    x1 = x_ref[...] + _dot(d_ref[...].astype(BF16), wo_ref[...])
    h = _rms(x1, g_ref[...]).astype(BF16)
    acc = x1
    for c0, c1 in chunks:
        g = _dot(h, wg_ref[:, c0:c1])
        u = _dot(h, wu_ref[:, c0:c1])
        a = (g / (1.0 + jnp.exp(-g)) * u).astype(BF16)
        acc = acc + _dot(a, wd_ref[c0:c1, :])
    o_ref[...] = _rms(acc, fg_ref[...]) if final else acc


def _post(x2d, d2d, wo, ffn_gain, wg, wu, wd, final_gain, final):
    n, dm = x2d.shape
    f = wg.shape[1]
    tm = _row_tile(n)
    step = 1024
    chunks = tuple((c, min(c + step, f)) for c in range(0, f, step))
    row = lambda i: (i, 0)
    return pl.pallas_call(
        functools.partial(_post_kernel, chunks=chunks, final=final),
        grid=(n // tm,),
        in_specs=[pl.BlockSpec((tm, dm), row), pl.BlockSpec((tm, dm), row), _resident(wo.shape),
                  _resident((1, dm)), _resident(wg.shape), _resident(wu.shape), _resident(wd.shape),
                  _resident((1, dm))],
        out_specs=pl.BlockSpec((tm, dm), row),
        out_shape=jax.ShapeDtypeStruct((n, dm), F32),
        compiler_params=_params("parallel"),
        name="post_final" if final else "post",
    )(x2d, d2d, wo, ffn_gain.reshape(1, dm), wg, wu, wd, final_gain.reshape(1, dm))


def _log_gates(g, is_forget):
    ig = GATE_SOFTCAP * jnp.tanh(g / GATE_SOFTCAP)
    lf = jnp.minimum(g, 0.0) - jnp.log1p(jnp.exp(-jnp.abs(g)))
    return jnp.where(is_forget, lf, ig)


def _mlstm_proj_kernel(x_ref, g_ref, w_ref, wkt_ref, wgt_ref, bc_ref, br_ref,
                       q_ref, kt_ref, v_ref, og_ref, gc_ref, gr_ref, *, hk, hv, q_scale):
    h = _rms(x_ref[...], g_ref[...]).astype(BF16)
    q_ref[...] = _dot(h, w_ref[:, :hk]) * q_scale
    kt_ref[...] = _nt_dot(wkt_ref[...], h)
    v_ref[...] = _dot(h, w_ref[:, 2 * hk:2 * hk + hv])
    og_ref[...] = _dot(h, w_ref[:, 2 * hk + hv:])
    wgt = wgt_ref[...]
    ng = wgt.shape[0]
    gcol = lax.dot_general(h, wgt, (((1,), (1,)), ((), ())), preferred_element_type=F32) + bc_ref[...]
    grow = _nt_dot(wgt, h) + br_ref[...]
    gc_ref[...] = _log_gates(gcol, lax.broadcasted_iota(jnp.int32, gcol.shape, 1) >= ng // 2)
    gr_ref[...] = _log_gates(grow, lax.broadcasted_iota(jnp.int32, grow.shape, 0) >= ng // 2)


def _mlstm_proj(x2d, gain, w_main, w_k_t, w_gates_t, b_gates, hk, hv, q_scale):
    n, d = x2d.shape
    tm = _row_tile(n)
    ng = w_gates_t.shape[0]
    row = lambda i: (i, 0)
    col = lambda i: (0, i)
    return pl.pallas_call(
        functools.partial(_mlstm_proj_kernel, hk=hk, hv=hv, q_scale=q_scale),
        grid=(n // tm,),
        in_specs=[pl.BlockSpec((tm, d), row), _resident((1, d)), _resident(w_main.shape),
                  _resident(w_k_t.shape), _resident(w_gates_t.shape), _resident((1, ng)), _resident((ng, 1))],
        out_specs=[pl.BlockSpec((tm, hk), row), pl.BlockSpec((hk, tm), col), pl.BlockSpec((tm, hv), row),
                   pl.BlockSpec((tm, d), row), pl.BlockSpec((tm, ng), row), pl.BlockSpec((ng, tm), col)],
        out_shape=[jax.ShapeDtypeStruct((n, hk), F32), jax.ShapeDtypeStruct((hk, n), F32),
                   jax.ShapeDtypeStruct((n, hv), F32), jax.ShapeDtypeStruct((n, d), F32),
                   jax.ShapeDtypeStruct((n, ng), F32), jax.ShapeDtypeStruct((ng, n), F32)],
        compiler_params=_params("parallel"),
        name="mlstm_proj",
    )(x2d, gain.reshape(1, d), w_main, w_k_t, w_gates_t, b_gates.reshape(1, ng), b_gates.reshape(ng, 1))


def _split3(x):
    hi = x.astype(BF16)
    r1 = x - hi.astype(F32)
    mid = r1.astype(BF16)
    lo = (r1 - mid.astype(F32)).astype(BF16)
    return hi, mid, lo


def _mlstm_kernel(q_ref, kt_ref, v_ref, og_ref, gc_ref, gr_ref, gain_ref, s0_ref, m0_ref,
                  y_ref, s_out_ref, m_out_ref, s_sc, m_sc, *, heads):
    c = pl.program_id(1)
    L = q_ref.shape[0]
    dk = q_ref.shape[1] // heads
    dk2 = 2 * dk
    dv = v_ref.shape[1] // heads

    @pl.when(c == 0)
    def _():
        for pr in range(heads // 2):
            s_sc[pr] = jnp.concatenate([s0_ref[0, 2 * pr], s0_ref[0, 2 * pr + 1]], axis=0)
        m_sc[...] = m0_ref[0]

    tt = lax.broadcasted_iota(jnp.int32, (L, L), 0)
    ss = lax.broadcasted_iota(jnp.int32, (L, L), 1)
    causal = ss <= tt
    tril = jnp.where(causal, 1.0, 0.0).astype(BF16)
    triu = jnp.where(tt <= ss, 1.0, 0.0).astype(BF16)

    gc = gc_ref[...]
    gr = gr_ref[0]
    bcol = sum(_dot(tril, part) for part in _split3(gc))
    brow = sum(_dot(part, triu) for part in _split3(gr))
    lane = lax.broadcasted_iota(jnp.int32, (L, dk2), 1)
    sub = lax.broadcasted_iota(jnp.int32, (dk2, L), 0)
    one_col = jnp.where(lax.broadcasted_iota(jnp.int32, (L, dv), 1) == 0, 1.0, 0.0)
    last = c == pl.num_programs(1) - 1

    H = range(heads)
    P = range(heads // 2)
    qs = [q_ref[:, pr * dk2:(pr + 1) * dk2].astype(BF16) for pr in P]
    kts = [kt_ref[0, pr * dk2:(pr + 1) * dk2, :].astype(BF16) for pr in P]
    i_col = [gc[:, h:h + 1] for h in H]
    b_col = [bcol[:, heads + h:heads + h + 1] for h in H]
    i_row = [gr[h:h + 1, :] for h in H]
    b_row = [brow[heads + h:heads + h + 1, :] for h in H]
    b_last = [b[:, L - 1:L] for b in b_row]

    u = [jnp.where(causal, i_row[h] - b_row[h], -jnp.inf) for h in H]
    cm = [jnp.max(x, axis=-1, keepdims=True) for x in u]
    e = [jnp.exp(u[h] - cm[h]) for h in H]
    a = [b_col[h] + cm[h] for h in H]
    qz = [jnp.where((lane >= (h % 2) * dk) & (lane < (h % 2 + 1) * dk), qs[h // 2], jnp.zeros_like(qs[0]))
          for h in H]
    qk = [_dot(qz[h], kts[h // 2]) for h in H]
    g = [(qk[h] * e[h]).astype(BF16) for h in H]
    vx = [jnp.concatenate([v_ref[:, h * dv:(h + 1) * dv], one_col], axis=1) for h in H]
    intra = [_dot(g[h], vx[h].astype(BF16)) for h in H]

    m_loc = [jnp.max(b_last[h] - b_row[h] + i_row[h], axis=-1, keepdims=True) for h in H]
    ws_col = [jnp.exp(b_last[h] - b_col[h] + i_col[h] - m_loc[h]) for h in H]
    ktz = [jnp.where((sub >= (h % 2) * dk) & (sub < (h % 2 + 1) * dk), kts[h // 2], jnp.zeros_like(kts[0]))
           for h in H]
    upd = [_dot(ktz[h], (ws_col[h] * vx[h]).astype(BF16)) for h in H]

    state = [s_sc[pr] for pr in P]
    state_bf = [x.astype(BF16) for x in state]
    inter_c = [_dot(qz[h], state_bf[h // 2]) for h in H]
    m_prev = [m_sc[h][:, :1] for h in H]
    inter = [b_col[h] + m_prev[h] for h in H]
    mt = [jnp.maximum(inter[h], a[h]) for h in H]
    nd = [jnp.exp(inter[h] - mt[h]) * inter_c[h] + jnp.exp(a[h] - mt[h]) * intra[h] for h in H]
    inv = [1.0 / jnp.maximum(jnp.abs(nd[h][:, dv:dv + 1]), jnp.exp(-mt[h])) for h in H]
    msq = [jnp.mean(nd[h][:, :dv] * nd[h][:, :dv], axis=-1, keepdims=True) for h in H]
    scale = [inv[h] * lax.rsqrt(msq[h] * inv[h] * inv[h] + RMS_EPS) for h in H]
    sig = [1.0 + jnp.exp(-og_ref[:, h * dv:(h + 1) * dv]) for h in H]
    for h in H:
        y_ref[:, h * dv:(h + 1) * dv] = (nd[h][:, :dv] * scale[h] * gain_ref[h] / sig[h]).astype(y_ref.dtype)

    m_new = [jnp.maximum(b_last[h] + m_prev[h], m_loc[h]) for h in H]
    gain_new = [jnp.exp(m_loc[h] - m_new[h]) for h in H]
    decay = [jnp.exp(b_last[h] + m_prev[h] - m_new[h]) for h in H]
    row_id = lax.broadcasted_iota(jnp.int32, state[0].shape, 0)
    for pr in P:
        h0, h1 = 2 * pr, 2 * pr + 1
        s_sc[pr] = (gain_new[h0] * upd[h0] + gain_new[h1] * upd[h1]
                    + jnp.where(row_id < dk, decay[h0], decay[h1]) * state[pr])
    for h in H:
        m_sc[h] = jnp.broadcast_to(m_new[h], m_sc.shape[1:])

    @pl.when(last)
    def _():
        for pr in range(heads // 2):
            s_out_ref[0, 2 * pr] = s_sc[pr, :dk]
            s_out_ref[0, 2 * pr + 1] = s_sc[pr, dk:]
        m_out_ref[0] = m_sc[...]


def _mlstm(q, kt, v, og, gc, gr, head_gain, s0, m0, batch, nc):
    n, hk = q.shape
    heads, dv = head_gain.shape
    dk = hk // heads
    L = n // (batch * nc)
    ng = gc.shape[1]
    blk = lambda b, c: (b * nc + c, 0)
    st = lambda b, c: (b, 0, 0, 0)
    if kt.shape[0] == 1:
        lanes_blk = lambda b, c: (0, 0, b * nc + c)
    else:
        lanes_blk = lambda b, c: (b * nc + c, 0, 0)
    y, s_out, m_out = pl.pallas_call(
        functools.partial(_mlstm_kernel, heads=heads),
        grid=(batch, nc),
        in_specs=[
            pl.BlockSpec((L, hk), blk), pl.BlockSpec((1, hk, L), lanes_blk),
            pl.BlockSpec((L, heads * dv), blk), pl.BlockSpec((L, heads * dv), blk),
            pl.BlockSpec((L, ng), blk),
            pl.BlockSpec((1, ng, L), lanes_blk),
            pl.BlockSpec((heads, 1, dv), lambda b, c: (0, 0, 0)),
            pl.BlockSpec((1, heads, dk, 2 * dv), st),
            pl.BlockSpec((1, heads, 1, LANES), st),
        ],
        out_specs=[pl.BlockSpec((L, heads * dv), blk), pl.BlockSpec((1, heads, dk, 2 * dv), st),
                   pl.BlockSpec((1, heads, 1, LANES), st)],
        out_shape=[jax.ShapeDtypeStruct((n, heads * dv), BF16),
                   jax.ShapeDtypeStruct((batch, heads, dk, 2 * dv), F32),
                   jax.ShapeDtypeStruct((batch, heads, 1, LANES), F32)],
        scratch_shapes=[pltpu.VMEM((heads // 2, 2 * dk, 2 * dv), F32), pltpu.VMEM((heads, 1, LANES), F32)],
        compiler_params=_params("parallel", "arbitrary"),
        name="mlstm",
    )(q, kt, v, og, gc, gr, head_gain.reshape(heads, 1, dv), s0, m0)
    return y, s_out[..., :dv], s_out[..., dv], m_out[:, :, 0, 0]


def _pack_state(C, nvec, m):
    dv = C.shape[-1]
    s0 = jnp.concatenate([C, nvec[..., None], jnp.zeros(C.shape[:-1] + (dv - 1,), F32)], axis=-1)
    return s0, jnp.broadcast_to(m[..., None, None], m.shape + (1, LANES))


def _pad_chunks(a, batch, t, L, value=0.0):
    w = a.shape[-1]
    return jnp.pad(a.reshape(batch, t, w), ((0, 0), (0, L - t), (0, 0)), constant_values=value).reshape(batch * L, w)


def _mlstm_layer(x2d, batch, t, norm_gain, w_main, w_k_t, w_gates_t, b_gates, head_gain, C0, n0, m0):
    heads, dv = head_gain.shape
    ng = 2 * heads
    hk = w_k_t.shape[0]
    dk = hk // heads
    q, kt, v, og, gc, gr = _mlstm_proj(x2d, norm_gain, w_main, w_k_t, w_gates_t, b_gates, hk, heads * dv,
                                       dk ** -0.5)
    s0, m0p = _pack_state(C0, n0, m0)
    if t % ML_CHUNK == 0:
        y, C, nvec, m = _mlstm(q, kt[None], v, og, gc, gr[None], head_gain, s0, m0p, batch, t // ML_CHUNK)
        return y, C, nvec, m
    L = -(-t // SHORT_CHUNK) * SHORT_CHUNK
    pad = lambda a: _pad_chunks(a, batch, t, L)
    gate_pad = jnp.concatenate([jnp.full((heads,), NEG_BIG, F32), jnp.zeros((heads,), F32)])
    gcp = jnp.where((jnp.arange(batch * L) % L < t)[:, None], pad(gc), gate_pad[None, :])
    grp = gcp.reshape(batch, L, ng).transpose(0, 2, 1)
    ktp = jnp.pad(kt.reshape(hk, batch, t).transpose(1, 0, 2), ((0, 0), (0, 0), (0, L - t)))
    y, C, nvec, m = _mlstm(pad(q), ktp, pad(v), pad(og), gcp, grp, head_gain, s0, m0p, batch, 1)
    y = y.reshape(batch, L, -1)[:, :t].reshape(batch * t, -1)
    return y, C, nvec, m


def kernel(x_prompt, x_sample, cache_k, cache_v, page_table, state_C, state_n, state_m, attn_norm, w_attn_qkv, attn_lambda, attn_head_gain, w_attn_out, mlstm_norm, w_mlstm_in, b_mlstm_gates, mlstm_head_gain, w_mlstm_out, ffn_norm, w_ffn_gate, w_ffn_up, w_ffn_down, final_norm):
    bp, tp, dm = x_prompt.shape
    bs, ts, _ = x_sample.shape
    depth = ffn_norm.shape[0]
    n_mixers = 2
    da_heads, da_vd = attn_head_gain.shape[1:]
    ml_heads, ml_dv = mlstm_head_gain.shape[1:]
    nqk = (w_attn_qkv.shape[2] - da_heads * da_vd) // 2
    da_hd = nqk // (2 * da_heads)
    past_len = page_table.shape[1] * cache_k.shape[2]

    tab_p, tab_pt = _rope_tables(jnp.arange(tp, dtype=F32))
    tab_s = tuple(jnp.tile(a, (bs, 1)) for a in _rope_tables(past_len + jnp.arange(ts, dtype=F32))[0])

    xp = x_prompt.reshape(bp * tp, dm)
    xs = x_sample.reshape(bs * ts, dm)
    k_p, v_p, k_s, v_s = [], [], [], []
    c_p, n_p, m_p, c_s, n_s, m_s = [], [], [], [], [], []
    for i in range(depth):
        final = i == depth - 1
        if i % n_mixers == 0:
            a = i // n_mixers
            lam_init = 0.8 - 0.6 * math.exp(-0.3 * i)
            q_scale = da_hd ** -0.5 * LOG2E
            w_qkv = w_attn_qkv[a].astype(BF16)
            w_out = w_attn_out[a].astype(BF16)
            wkt = w_attn_qkv[a][:, nqk:2 * nqk].T.astype(BF16)
            q, kt, v = _attn_proj_kt(xp, attn_norm[a], w_qkv, wkt, tab_p, tab_pt, bp, nqk, da_hd,
                                     da_heads * da_vd, q_scale)
            qs, k, vs = _attn_proj(xs, attn_norm[a], w_qkv, tab_s, nqk, da_heads * da_vd, q_scale)
            if _fused_attention_fits(bp, tp, da_heads, bs, page_table.shape[1]):
                dp, ds = _fused_attention(q, kt, v, qs, k, vs, cache_k, cache_v, a, page_table, attn_lambda[a],
                                          attn_head_gain[a], bp, tp, ts, lam_init)
            else:
                dp = _flash_attention(q, kt, v, attn_lambda[a], attn_head_gain[a], bp, tp, lam_init)
                ds = _paged_attention(qs, k, vs, cache_k, cache_v, a, page_table, attn_lambda[a],
                                      attn_head_gain[a], ts, lam_init)
            k_p.append(kt.reshape(bp, 2 * da_heads, da_hd, tp).transpose(0, 3, 1, 2))
            v_p.append(v.reshape(bp, tp, da_heads, da_vd))
            k_s.append(k.reshape(bs, ts, 2 * da_heads, da_hd))
            v_s.append(vs.reshape(bs, ts, da_heads, da_vd))
        else:
            j = i // n_mixers
            hv = ml_heads * ml_dv
            w_in = w_mlstm_in[j]
            w_main = w_in[:, :w_in.shape[1] - 2 * ml_heads].astype(BF16)
            w_gates_t = w_in[:, w_in.shape[1] - 2 * ml_heads:].T.astype(BF16)
            w_out = w_mlstm_out[j].astype(BF16)
            hk = (w_main.shape[1] - hv - dm) // 2
            dk = hk // ml_heads
            w_k_t = w_in[:, hk:2 * hk].T.astype(BF16)
            zero = (jnp.zeros((bp, ml_heads, dk, ml_dv), F32), jnp.zeros((bp, ml_heads, dk), F32),
                    jnp.zeros((bp, ml_heads), F32))
            dp, C, nv, m = _mlstm_layer(xp, bp, tp, mlstm_norm[j], w_main, w_k_t, w_gates_t, b_mlstm_gates[j],
                                        mlstm_head_gain[j], *zero)
            c_p.append(C); n_p.append(nv); m_p.append(m)
            ds, C, nv, m = _mlstm_layer(xs, bs, ts, mlstm_norm[j], w_main, w_k_t, w_gates_t, b_mlstm_gates[j],
                                        mlstm_head_gain[j], state_C[j], state_n[j], state_m[j])
            c_s.append(C); n_s.append(nv); m_s.append(m)
        wg, wu, wd = (w[i].astype(BF16) for w in (w_ffn_gate, w_ffn_up, w_ffn_down))
        xp = _post(xp, dp, w_out, ffn_norm[i], wg, wu, wd, final_norm, final)
        xs = _post(xs, ds, w_out, ffn_norm[i], wg, wu, wd, final_norm, final)
    return (xp.reshape(bp, tp, dm), xs.reshape(bs, ts, dm),
            jnp.stack(k_p), jnp.stack(v_p), jnp.stack(c_p), jnp.stack(n_p), jnp.stack(m_p),
            jnp.stack(k_s), jnp.stack(v_s), jnp.stack(c_s), jnp.stack(n_s), jnp.stack(m_s))
```
